```python
import math
import jax, jax.numpy as jnp
from jax import lax
import numpy as np

D_MODEL = 1024
BATCH = 8
SEQ = 2048
DEPTH = 2

RET_HEADS = 4
RET_DK = 256
RET_DV = 512
RET_CHUNK = 128
MLA_HEADS = 8
MLA_NOPE = 128
MLA_ROPE = 64
MLA_V = 128
MLA_Q_RANK = 256
MLA_KV_RANK = 256
ATTN_Q_BLOCK = 128
MOBA_HEADS = 16
MOBA_DH = D_MODEL // MOBA_HEADS
MOBA_BLOCK = 256
MOBA_TOPK = 3
MOBA_Q_CHUNK = 32

ROPE_THETA = 10000.0
LN_EPS = 1e-5
RMS_EPS = 1e-6
NEG = -1e30
DN_ALPHA = (2.0 * DEPTH) ** 0.25
DN_BETA = (8.0 * DEPTH) ** -0.25
N_EVEN = (DEPTH + 1) // 2
N_ODD = DEPTH // 2

RET_W = RET_HEADS * RET_DV
MLA_W = MLA_HEADS * MLA_V
MOBA_W = MOBA_HEADS * MOBA_DH
EVEN_SPLITS = (RET_HEADS * RET_DK, RET_HEADS * RET_DK, RET_W, RET_W,
               MLA_Q_RANK, MLA_KV_RANK, MLA_ROPE, MLA_W)
EVEN_IN = sum(EVEN_SPLITS)
ODD_SPLITS = (MOBA_W, MOBA_W, MOBA_W, MOBA_W)
ODD_IN = sum(ODD_SPLITS)

kernel_name = "hybrid_retention_mla_moba_deepnorm"


def _split(h, sizes):
    offs = np.cumsum(sizes)[:-1].tolist()
    return jnp.split(h, offs, axis=-1)


def _heads(t, n):
    B, S, W = t.shape
    return t.reshape(B, S, n, W // n).transpose(0, 2, 1, 3)


def _merge(t):
    B, H, S, d = t.shape
    return t.transpose(0, 2, 1, 3).reshape(B, S, H * d)


def layer_norm(x, g, b):
    xf = x.astype(jnp.float32)
    mu = jnp.mean(xf, -1, keepdims=True)
    xc = xf - mu
    var = jnp.mean(xc * xc, -1, keepdims=True)
    return (xc * lax.rsqrt(var + LN_EPS) * g + b).astype(x.dtype)


def rms_norm(x, g):
    xf = x.astype(jnp.float32)
    return (xf * lax.rsqrt(jnp.mean(xf * xf, -1, keepdims=True) + RMS_EPS) * g).astype(x.dtype)


def rope(x, pos):
    d = x.shape[-1]
    inv = 1.0 / (ROPE_THETA ** (jnp.arange(0, d, 2, dtype=jnp.float32) / d))
    ang = pos.astype(jnp.float32)[:, None, :, None] * inv
    cos, sin = jnp.cos(ang), jnp.sin(ang)
    x1, x2 = x[..., : d // 2], x[..., d // 2:]
    return jnp.concatenate([x1 * cos - x2 * sin, x1 * sin + x2 * cos], -1).astype(x.dtype)


def retnet_rotate(x, pos):
    d = x.shape[-1]
    inv = 1.0 / (ROPE_THETA ** jnp.linspace(0.0, 1.0, d // 2, dtype=jnp.float32))
    ang = pos.astype(jnp.float32)[:, None, :, None] * inv
    cos, sin = jnp.cos(ang), jnp.sin(ang)
    x1, x2 = x[..., 0::2], x[..., 1::2]
    out = jnp.stack([x1 * cos - x2 * sin, x2 * cos + x1 * sin], -1)
    return out.reshape(x.shape).astype(x.dtype)


def retention_chunkwise(q, k, v):
    B, H, S, dk = q.shape
    dv = v.shape[-1]
    C = RET_CHUNK
    n = S // C
    q = q.astype(jnp.float32)
    k = k.astype(jnp.float32) * (dk ** -0.5)
    v = v.astype(jnp.float32)
    log_g = jnp.log(1.0 - 2.0 ** (-5.0 - jnp.arange(H, dtype=jnp.float32)))
    qc = q.reshape(B, H, n, C, dk)
    kc = k.reshape(B, H, n, C, dk)
    vc = v.reshape(B, H, n, C, dv)
    idx = jnp.arange(C, dtype=jnp.float32)
    rel = idx[:, None] - idx[None, :]
    decay = jnp.where(rel >= 0, jnp.exp(log_g[:, None, None] * jnp.maximum(rel, 0.0)), 0.0)
    inner = jnp.einsum('bhncd,bhnkd->bhnck', qc, kc) * decay[None, :, None]
    o_inner = jnp.einsum('bhnck,bhnkv->bhncv', inner, vc)
    zeta = jnp.exp(log_g[:, None] * (C - 1.0 - idx))
    kv_chunk = jnp.einsum('bhnkd,bhnkv->bhndv', kc * zeta[None, :, None, :, None], vc)
    g_chunk = jnp.exp(log_g * C)[None, :, None, None]

    def step(R, kv):
        return R * g_chunk + kv, R

    _, r_prev = lax.scan(step, jnp.zeros((B, H, dk, dv), jnp.float32), jnp.moveaxis(kv_chunk, 2, 0))
    r_prev = jnp.moveaxis(r_prev, 0, 2)
    xi = jnp.exp(log_g[:, None] * (idx + 1.0))
    o_cross = jnp.einsum('bhncd,bhndv->bhncv', qc, r_prev) * xi[None, :, None, :, None]
    return (o_inner + o_cross).reshape(B, H, S, dv)


def head_group_norm(o):
    of = o.astype(jnp.float32)
    mu = jnp.mean(of, -1, keepdims=True)
    oc = of - mu
    var = jnp.mean(oc * oc, -1, keepdims=True)
    return oc * lax.rsqrt(var + LN_EPS)


def causal_attention(q, k, v, scale):
    B, H, S, d = q.shape
    dv = v.shape[-1]
    nq = S // ATTN_Q_BLOCK
    qb = q.reshape(B, H, nq, ATTN_Q_BLOCK, d).transpose(2, 0, 1, 3, 4)
    kpos = jnp.arange(S)

    def one(args):
        qi, i = args
        s = jnp.einsum('bhqd,bhkd->bhqk', qi, k).astype(jnp.float32) * scale
        qpos = i * ATTN_Q_BLOCK + jnp.arange(ATTN_Q_BLOCK)
        s = jnp.where(kpos[None, :] <= qpos[:, None], s, NEG)
        p = jax.nn.softmax(s, axis=-1).astype(v.dtype)
        return jnp.einsum('bhqk,bhkd->bhqd', p, v)

    out = lax.map(one, (qb, jnp.arange(nq)))
    return out.transpose(1, 2, 0, 3, 4).reshape(B, H, S, dv)


def moba_attention(q, k, v):
    B, H, S, dh = q.shape
    L = MOBA_BLOCK
    nb = -(-S // L)
    pad = nb * L - S
    kp = jnp.pad(k, ((0, 0), (0, 0), (0, pad), (0, 0)))
    vp = jnp.pad(v, ((0, 0), (0, 0), (0, pad), (0, 0)))
    kb = kp.reshape(B, H, nb, L, dh)
    vb = vp.reshape(B, H, nb, L, dh)
    kmean = jnp.mean(kb.astype(jnp.float32), axis=3)
    qblk = jnp.arange(S) // L
    gate = jnp.einsum('bhsd,bhnd->bhsn', q.astype(jnp.float32), kmean)
    past = jnp.arange(nb)[None, :] < qblk[:, None]
    gate = jnp.where(past, gate, NEG)
    k_sel = min(MOBA_TOPK, nb)
    _, sel = lax.top_k(gate, k_sel)
    valid = jnp.arange(k_sel)[None, :] < qblk[:, None]
    scale = dh ** -0.5
    Qc = MOBA_Q_CHUNK
    nc = S // Qc
    qch = jnp.moveaxis(q.reshape(B, H, nc, Qc, dh), 2, 0)
    selch = jnp.moveaxis(sel.reshape(B, H, nc, Qc, k_sel), 2, 0)
    validch = valid.reshape(nc, Qc, k_sel)
    bi = jnp.arange(B)[:, None, None, None]
    hi = jnp.arange(H)[None, :, None, None]

    def one(args):
        qi, si, vi, c = args
        q0 = c * Qc
        qpos = q0 + jnp.arange(Qc)
        own = q0 // L
        kg = kb[bi, hi, si]
        vg = vb[bi, hi, si]
        s_g = jnp.einsum('bhqd,bhqnld->bhqnl', qi, kg).astype(jnp.float32) * scale
        s_g = jnp.where(vi[None, None, :, :, None], s_g, NEG)
        ko = lax.dynamic_slice_in_dim(kp, own * L, L, axis=2)
        vo = lax.dynamic_slice_in_dim(vp, own * L, L, axis=2)
        s_o = jnp.einsum('bhqd,bhld->bhql', qi, ko).astype(jnp.float32) * scale
        kpos = own * L + jnp.arange(L)
        s_o = jnp.where(kpos[None, :] <= qpos[:, None], s_o, NEG)
        s = jnp.concatenate([s_g.reshape(B, H, Qc, k_sel * L), s_o], axis=-1)
        p = jax.nn.softmax(s, axis=-1).astype(v.dtype)
        p_g = p[..., : k_sel * L].reshape(B, H, Qc, k_sel, L)
        p_o = p[..., k_sel * L:]
        return (jnp.einsum('bhqnl,bhqnld->bhqd', p_g, vg)
                + jnp.einsum('bhql,bhld->bhqd', p_o, vo))

    out = lax.map(one, (qch, selch, validch, jnp.arange(nc)))
    return jnp.moveaxis(out, 0, 2).reshape(B, H, S, dh)


def retention_mla_layer(x, pos, w_in, q_norm_g, w_uq, kv_norm_g, w_ukv, w_out):
    B, S, _ = x.shape
    h = x @ w_in
    rq, rk, rv, rg, mq, mkv, mkr, mg = _split(h, EVEN_SPLITS)
    q_r = retnet_rotate(_heads(rq, RET_HEADS), pos)
    k_r = retnet_rotate(_heads(rk, RET_HEADS), pos)
    o_r = head_group_norm(retention_chunkwise(q_r, k_r, _heads(rv, RET_HEADS)))
    ret_out = (_merge(o_r) * jax.nn.silu(rg.astype(jnp.float32))).astype(x.dtype)
    q_m = (rms_norm(mq, q_norm_g) @ w_uq).reshape(B, S, MLA_HEADS, MLA_NOPE + MLA_ROPE).transpose(0, 2, 1, 3)
    q_nope, q_pe = q_m[..., :MLA_NOPE], rope(q_m[..., MLA_NOPE:], pos)
    kv = (rms_norm(mkv, kv_norm_g) @ w_ukv).reshape(B, S, MLA_HEADS, MLA_NOPE + MLA_V).transpose(0, 2, 1, 3)
    k_nope, v_m = kv[..., :MLA_NOPE], kv[..., MLA_NOPE:]
    k_pe = rope(mkr[:, None], pos)
    q_full = jnp.concatenate([q_nope, q_pe], -1)
    k_full = jnp.concatenate([k_nope, jnp.broadcast_to(k_pe, (B, MLA_HEADS, S, MLA_ROPE))], -1)
    o_m = causal_attention(q_full, k_full, v_m, (MLA_NOPE + MLA_ROPE) ** -0.5)
    mla_out = (_merge(o_m) * jax.nn.silu(mg)).astype(x.dtype)
    return jnp.concatenate([ret_out, mla_out], axis=-1) @ w_out


def moba_layer(x, pos, w_in, w_out):
    h = x @ w_in
    q, k, v, g = _split(h, ODD_SPLITS)
    q = rope(_heads(q, MOBA_HEADS), pos)
    k = rope(_heads(k, MOBA_HEADS), pos)
    o = moba_attention(q, k, _heads(v, MOBA_HEADS))
    return (_merge(o) * jax.nn.silu(g)) @ w_out


def setup_inputs(seed: int = 0) -> dict:
    key = jax.random.key(seed)
    ks = jax.random.split(key, 16)
    f32 = jnp.float32
    nrm = lambda k, shape, fan_in, gain=1.0: jax.random.normal(k, shape, f32) * (gain * fan_in ** -0.5)
    x = jax.random.normal(ks[0], (BATCH, SEQ, D_MODEL), f32)
    offs = jax.random.randint(ks[1], (BATCH, 1), 0, 1024, dtype=jnp.int32)
    positions = (offs + jnp.arange(SEQ, dtype=jnp.int32)[None, :]).astype(jnp.int32)
    return {
        "x": x,
        "positions": positions,
        "w_in_even": nrm(ks[2], (N_EVEN, D_MODEL, EVEN_IN), D_MODEL),
        "q_norm_even": 1.0 + 0.02 * jax.random.normal(ks[3], (N_EVEN, MLA_Q_RANK), f32),
        "w_uq_even": nrm(ks[4], (N_EVEN, MLA_Q_RANK, MLA_HEADS * (MLA_NOPE + MLA_ROPE)), MLA_Q_RANK),
        "kv_norm_even": 1.0 + 0.02 * jax.random.normal(ks[5], (N_EVEN, MLA_KV_RANK), f32),
        "w_ukv_even": nrm(ks[6], (N_EVEN, MLA_KV_RANK, MLA_HEADS * (MLA_NOPE + MLA_V)), MLA_KV_RANK),
        "w_out_even": nrm(ks[7], (N_EVEN, RET_W + MLA_W, D_MODEL), RET_W + MLA_W, DN_BETA),
        "w_in_odd": nrm(ks[8], (N_ODD, D_MODEL, ODD_IN), D_MODEL),
        "w_out_odd": nrm(ks[9], (N_ODD, MOBA_W, D_MODEL), MOBA_W, DN_BETA),
        "ln_g": 1.0 + 0.02 * jax.random.normal(ks[10], (DEPTH, D_MODEL), f32),
        "ln_b": 0.02 * jax.random.normal(ks[11], (DEPTH, D_MODEL), f32),
    }


def reference(x, positions, w_in_even, q_norm_even, w_uq_even, kv_norm_even, w_ukv_even,
              w_out_even, w_in_odd, w_out_odd, ln_g, ln_b):
    for i in range(DEPTH):
        j = i // 2
        if i % 2 == 0:
            y = retention_mla_layer(x, positions, w_in_even[j], q_norm_even[j], w_uq_even[j],
                                    kv_norm_even[j], w_ukv_even[j], w_out_even[j])
        else:
            y = moba_layer(x, positions, w_in_odd[j], w_out_odd[j])
        x = layer_norm(DN_ALPHA * x + y, ln_g[i], ln_b[i])
    return x
```

```python
import functools

import numpy as np
import jax
import jax.numpy as jnp
from jax import lax
from jax.experimental import pallas as pl
from jax.experimental.pallas import tpu as pltpu

D_MODEL = 1024
BATCH = 8
SEQ = 2048
DEPTH = 2
TOKENS = BATCH * SEQ

RET_HEADS = 4
RET_DK = 256
RET_DV = 512
MLA_HEADS = 8
MLA_NOPE = 128
MLA_ROPE = 64
MLA_V = 128
MLA_Q_RANK = 256
MLA_KV_RANK = 256
MOBA_HEADS = 16
MOBA_DH = 64
MOBA_BLOCK = 256
MOBA_TOPK = 3
MOBA_NB = SEQ // MOBA_BLOCK

ROPE_THETA = 10000.0
LN_EPS = 1e-5
RMS_EPS = 1e-6
NEG = -1e30
DN_ALPHA = (2.0 * DEPTH) ** 0.25

RET_W = RET_HEADS * RET_DV
MLA_W = MLA_HEADS * MLA_V
MOBA_W = MOBA_HEADS * MOBA_DH

LANES = 128
HALF = LANES // 2
VMEM_LIMIT = 48 * 1024 * 1024

MAIN_RQ, MAIN_RK, MAIN_RV, MAIN_RG, MAIN_MG = 0, 1024, 2048, 4096, 6144
MAIN_W = 7168
SMALL_W = MLA_Q_RANK + MLA_KV_RANK + LANES

PROJ_TM = 1024
PROJ_TN = 1024
RET_CHUNK = 256
ATT_TQ = 256
ATT_TK = 256
PREP_TM = 512
OUT_TM = 512
TAB_TM = 1024


def _cparams(sem):
    return pltpu.CompilerParams(dimension_semantics=sem, vmem_limit_bytes=VMEM_LIMIT)


def _silu(g):
    return g / (1.0 + jnp.exp(-g))


def _tables_kernel(pos_ref, inv_r_ref, inv_p_ref, cr_ref, sr_ref, cp_ref, sp_ref):
    pos = pos_ref[...].astype(jnp.float32)
    ang_r = pos * inv_r_ref[...]
    cr_ref[...] = jnp.cos(ang_r)
    sr_ref[...] = jnp.sin(ang_r)
    ang_p = pos * inv_p_ref[...]
    lane = lax.broadcasted_iota(jnp.int32, ang_p.shape, 1)
    cp_ref[...] = jnp.cos(ang_p)
    sin_p = jnp.sin(ang_p)
    sp_ref[...] = jnp.where(lane < HALF, -sin_p, sin_p)


def _tables(pos, inv_r, inv_p):
    tab = jax.ShapeDtypeStruct((TOKENS, LANES), jnp.float32)
    row = pl.BlockSpec((TAB_TM, LANES), lambda i: (i, 0))
    cst = pl.BlockSpec((1, LANES), lambda i: (0, 0))
    return pl.pallas_call(
        _tables_kernel,
        grid=(TOKENS // TAB_TM,),
        in_specs=[pl.BlockSpec((TAB_TM, 1), lambda i: (i, 0)), cst, cst],
        out_specs=[row, row, row, row],
        out_shape=[tab, tab, tab, tab],
        compiler_params=_cparams(("parallel",)),
        name="rot_tables",
    )(pos, inv_r, inv_p)


def _proj_kernel(x_ref, w_ref, cos_ref, sin_ref, o_ref, xb_ref, *, mode, n_rot):
    j = pl.program_id(1)

    @pl.when(j == 0)
    def _():
        xb_ref[...] = x_ref[...].astype(jnp.bfloat16)

    acc = jnp.dot(xb_ref[...], w_ref[...], preferred_element_type=jnp.float32)

    @pl.when(j >= n_rot)
    def _():
        o_ref[...] = acc.astype(o_ref.dtype)

    @pl.when(j < n_rot)
    def _():
        cos = cos_ref[...]
        sin = sin_ref[...]
        if mode == "retnet":
            scale = jnp.where(j == 1, RET_DK ** -0.5, 1.0).astype(jnp.float32)
            for h in range(PROJ_TN // RET_DK):
                x1 = acc[:, h * RET_DK: h * RET_DK + LANES]
                x2 = acc[:, h * RET_DK + LANES: (h + 1) * RET_DK]
                o_ref[:, h * RET_DK: h * RET_DK + LANES] = ((x1 * cos - x2 * sin) * scale).astype(o_ref.dtype)
                o_ref[:, h * RET_DK + LANES: (h + 1) * RET_DK] = ((x2 * cos + x1 * sin) * scale).astype(o_ref.dtype)
        else:
            for g in range(PROJ_TN // LANES):
                xg = acc[:, g * LANES: (g + 1) * LANES]
                rot = xg * cos + pltpu.roll(xg, HALF, axis=1) * sin
                o_ref[:, g * LANES: (g + 1) * LANES] = rot.astype(o_ref.dtype)


def _project(x, w, cos, sin, *, mode, n_rot, name):
    n = w.shape[1]
    return pl.pallas_call(
        functools.partial(_proj_kernel, mode=mode, n_rot=n_rot),
        grid=(TOKENS // PROJ_TM, n // PROJ_TN),
        in_specs=[
            pl.BlockSpec((PROJ_TM, D_MODEL), lambda i, j: (i, 0)),
            pl.BlockSpec((D_MODEL, PROJ_TN), lambda i, j: (0, j)),
            pl.BlockSpec((PROJ_TM, LANES), lambda i, j: (i, 0)),
            pl.BlockSpec((PROJ_TM, LANES), lambda i, j: (i, 0)),
        ],
        out_specs=pl.BlockSpec((PROJ_TM, PROJ_TN), lambda i, j: (i, j)),
        out_shape=jax.ShapeDtypeStruct((TOKENS, n), jnp.bfloat16),
        scratch_shapes=[pltpu.VMEM((PROJ_TM, D_MODEL), jnp.bfloat16)],
        compiler_params=_cparams(("parallel", "arbitrary")),
        name=name,
    )(x, w, cos, sin)


def _retention_kernel(q_ref, k_ref, v_ref, g_ref, o_ref, state_ref):
    c = pl.program_id(2)
    C = RET_CHUNK

    @pl.when(c == 0)
    def _():
        state_ref[...] = jnp.zeros_like(state_ref)

    def log_gamma(shape):
        hf = jnp.full(shape, pl.program_id(1), jnp.int32).astype(jnp.float32)
        return jnp.log(1.0 - jnp.exp2(-5.0 - hf))

    q = q_ref[...]
    k = k_ref[...]
    v = v_ref[...]

    row = lax.broadcasted_iota(jnp.int32, (C, C), 0)
    col = lax.broadcasted_iota(jnp.int32, (C, C), 1)
    rel = (row - col).astype(jnp.float32)
    decay = jnp.where(rel >= 0, jnp.exp(log_gamma((C, C)) * jnp.maximum(rel, 0.0)), 0.0)

    s = lax.dot_general(q, k, (((1,), (1,)), ((), ())), preferred_element_type=jnp.float32)
    inner = (s * decay).astype(jnp.bfloat16)
    o = jnp.dot(inner, v, preferred_element_type=jnp.float32)

    state = state_ref[...]
    idx_v = lax.broadcasted_iota(jnp.int32, (C, RET_DV), 0).astype(jnp.float32)
    xi = jnp.exp(log_gamma((C, RET_DV)) * (idx_v + 1.0))
    o = o + jnp.dot(q, state.astype(jnp.bfloat16), preferred_element_type=jnp.float32) * xi

    idx_k = lax.broadcasted_iota(jnp.int32, (C, RET_DK), 0).astype(jnp.float32)
    zeta = jnp.exp(log_gamma((C, RET_DK)) * (C - 1.0 - idx_k))
    kz = (k.astype(jnp.float32) * zeta).astype(jnp.bfloat16)
    kv = lax.dot_general(kz, v, (((0,), (0,)), ((), ())), preferred_element_type=jnp.float32)
    g_chunk = jnp.exp(log_gamma((RET_DK, RET_DV)) * float(C))
    state_ref[...] = state * g_chunk + kv

    mu = jnp.mean(o, axis=-1, keepdims=True)
    oc = o - mu
    var = jnp.mean(oc * oc, axis=-1, keepdims=True)
    on = oc * lax.rsqrt(var + LN_EPS)
    o_ref[...] = (on * _silu(g_ref[...].astype(jnp.float32))).astype(o_ref.dtype)


def _retention(h_main):
    nc = SEQ // RET_CHUNK
    C = RET_CHUNK
    return pl.pallas_call(
        _retention_kernel,
        grid=(BATCH, RET_HEADS, nc),
        in_specs=[
            pl.BlockSpec((C, RET_DK), lambda b, h, c: (b * nc + c, MAIN_RQ // RET_DK + h)),
            pl.BlockSpec((C, RET_DK), lambda b, h, c: (b * nc + c, MAIN_RK // RET_DK + h)),
            pl.BlockSpec((C, RET_DV), lambda b, h, c: (b * nc + c, MAIN_RV // RET_DV + h)),
            pl.BlockSpec((C, RET_DV), lambda b, h, c: (b * nc + c, MAIN_RG // RET_DV + h)),
        ],
        out_specs=pl.BlockSpec((C, RET_DV), lambda b, h, c: (b * nc + c, h)),
        out_shape=jax.ShapeDtypeStruct((TOKENS, RET_W), jnp.bfloat16),
        scratch_shapes=[pltpu.VMEM((RET_DK, RET_DV), jnp.float32)],
        compiler_params=_cparams(("parallel", "parallel", "arbitrary")),
        name="retention",
    )(h_main, h_main, h_main, h_main)


def _rms_norm(x, g):
    return x * lax.rsqrt(jnp.mean(x * x, axis=-1, keepdims=True) + RMS_EPS) * g


def _mla_prep_kernel(x_ref, ws_ref, qg_ref, kvg_ref, wuq_ref, wukv_ref, cos_ref, sin_ref,
                     q_ref, k_ref, v_ref):
    xb = x_ref[...].astype(jnp.bfloat16)
    hs = jnp.dot(xb, ws_ref[...], preferred_element_type=jnp.float32)
    mq = hs[:, :MLA_Q_RANK]
    mkv = hs[:, MLA_Q_RANK: MLA_Q_RANK + MLA_KV_RANK]
    mkr = hs[:, MLA_Q_RANK + MLA_KV_RANK:]
    cos = cos_ref[...]
    sin = sin_ref[...]

    qn = _rms_norm(mq, qg_ref[...]).astype(jnp.bfloat16)
    kvn = _rms_norm(mkv, kvg_ref[...]).astype(jnp.bfloat16)
    qm = jnp.dot(qn, wuq_ref[...], preferred_element_type=jnp.float32)
    kv = jnp.dot(kvn, wukv_ref[...], preferred_element_type=jnp.float32)

    kpe = (mkr * cos + pltpu.roll(mkr, HALF, axis=1) * sin).astype(k_ref.dtype)
    scale = (MLA_NOPE + MLA_ROPE) ** -0.5
    lane = lax.broadcasted_iota(jnp.int32, (x_ref.shape[0], LANES), 1)
    second = (lane // (HALF // 2)) % 2
    nope_w = MLA_HEADS * MLA_NOPE
    for p in range(MLA_HEADS // 2):
        pe = qm[:, nope_w + p * LANES: nope_w + (p + 1) * LANES]
        pe = (pe * cos + pltpu.roll(pe, HALF, axis=1) * sin) * scale
        for e in range(2):
            h = 2 * p + e
            base = h * 2 * LANES
            q_ref[:, base: base + LANES] = (qm[:, h * LANES: (h + 1) * LANES] * scale).astype(q_ref.dtype)
            q_ref[:, base + LANES: base + 2 * LANES] = jnp.where(second == e, pe, 0.0).astype(q_ref.dtype)
            k_ref[:, base: base + LANES] = kv[:, h * LANES: (h + 1) * LANES].astype(k_ref.dtype)
            k_ref[:, base + LANES: base + 2 * LANES] = kpe
    v_ref[...] = kv[:, nope_w:].astype(v_ref.dtype)


def _mla_prep(x, w_small, q_g, kv_g, w_uq, w_ukv, cos_p, sin_p):
    tm = PREP_TM
    full = lambda a: pl.BlockSpec(a.shape, lambda i: (0, 0))
    qk_w = MLA_HEADS * 2 * LANES
    return pl.pallas_call(
        _mla_prep_kernel,
        grid=(TOKENS // tm,),
        in_specs=[
            pl.BlockSpec((tm, D_MODEL), lambda i: (i, 0)),
            full(w_small), full(q_g), full(kv_g), full(w_uq), full(w_ukv),
            pl.BlockSpec((tm, LANES), lambda i: (i, 0)),
            pl.BlockSpec((tm, LANES), lambda i: (i, 0)),
        ],
        out_specs=[
            pl.BlockSpec((tm, qk_w), lambda i: (i, 0)),
            pl.BlockSpec((tm, qk_w), lambda i: (i, 0)),
            pl.BlockSpec((tm, MLA_W), lambda i: (i, 0)),
        ],
        out_shape=[
            jax.ShapeDtypeStruct((TOKENS, qk_w), jnp.bfloat16),
            jax.ShapeDtypeStruct((TOKENS, qk_w), jnp.bfloat16),
            jax.ShapeDtypeStruct((TOKENS, MLA_W), jnp.bfloat16),
        ],
        compiler_params=_cparams(("parallel",)),
        name="mla_prep",
    )(x, w_small, q_g, kv_g, w_uq, w_ukv, cos_p, sin_p)


def _softmax_block(q, kj, vj, mask, m, l, acc):
    s = lax.dot_general(q, kj, (((1,), (1,)), ((), ())), preferred_element_type=jnp.float32)
    if mask is not None:
        s = jnp.where(mask, s, NEG)
    m_new = jnp.maximum(m, jnp.max(s, axis=-1, keepdims=True))
    p = jnp.exp(s - m_new)
    alpha = jnp.exp(m - m_new)
    l = alpha * l + jnp.sum(p, axis=-1, keepdims=True)
    acc = alpha * acc + jnp.dot(p.astype(vj.dtype), vj, preferred_element_type=jnp.float32)
    return m_new, l, acc


def _mla_attn_kernel(q_ref, k_ref, v_ref, g_ref, o_ref):
    i = pl.program_id(2)
    q = q_ref[...]
    row = lax.broadcasted_iota(jnp.int32, (ATT_TQ, ATT_TK), 0)
    col = lax.broadcasted_iota(jnp.int32, (ATT_TQ, ATT_TK), 1)

    d0 = pl.multiple_of(i * ATT_TK, ATT_TK)
    m0 = jnp.full((ATT_TQ, 1), NEG, jnp.float32)
    l0 = jnp.zeros((ATT_TQ, 1), jnp.float32)
    a0 = jnp.zeros((ATT_TQ, MLA_V), jnp.float32)
    carry = _softmax_block(q, k_ref[pl.ds(d0, ATT_TK), :], v_ref[pl.ds(d0, ATT_TK), :], col <= row, m0, l0, a0)

    def body(j, carry):
        j0 = pl.multiple_of(j * ATT_TK, ATT_TK)
        return _softmax_block(q, k_ref[pl.ds(j0, ATT_TK), :], v_ref[pl.ds(j0, ATT_TK), :], None, *carry)

    m, l, acc = lax.fori_loop(0, i, body, carry)
    o_ref[...] = (acc / l * _silu(g_ref[...].astype(jnp.float32))).astype(o_ref.dtype)


def _mla_attention(q_full, k_full, v_m, h_main):
    nq = SEQ // ATT_TQ
    qk_d = 2 * LANES
    return pl.pallas_call(
        _mla_attn_kernel,
        grid=(BATCH, MLA_HEADS, nq),
        in_specs=[
            pl.BlockSpec((ATT_TQ, qk_d), lambda b, h, i: (b * nq + i, h)),
            pl.BlockSpec((SEQ, qk_d), lambda b, h, i: (b, h)),
            pl.BlockSpec((SEQ, MLA_V), lambda b, h, i: (b, h)),
            pl.BlockSpec((ATT_TQ, MLA_V), lambda b, h, i: (b * nq + i, MAIN_MG // MLA_V + h)),
        ],
        out_specs=pl.BlockSpec((ATT_TQ, MLA_V), lambda b, h, i: (b * nq + i, h)),
        out_shape=jax.ShapeDtypeStruct((TOKENS, MLA_W), jnp.bfloat16),
        compiler_params=_cparams(("parallel", "parallel", "arbitrary")),
        name="mla_attention",
    )(q_full, k_full, v_m, h_main)


def _out_ln_kernel(*refs, n_pairs):
    a_refs = refs[:n_pairs]
    w_refs = refs[n_pairs: 2 * n_pairs]
    x_ref, g_ref, b_ref, o_ref = refs[2 * n_pairs:]
    y = jnp.dot(a_refs[0][...], w_refs[0][...], preferred_element_type=jnp.float32)
    for a_ref, w_ref in zip(a_refs[1:], w_refs[1:]):
        y = y + jnp.dot(a_ref[...], w_ref[...], preferred_element_type=jnp.float32)
    z = DN_ALPHA * x_ref[...] + y
    mu = jnp.mean(z, axis=-1, keepdims=True)
    zc = z - mu
    var = jnp.mean(zc * zc, axis=-1, keepdims=True)
    o_ref[...] = zc * lax.rsqrt(var + LN_EPS) * g_ref[...] + b_ref[...]


def _out_ln(acts, weights, x, ln_g, ln_b, name):
    tm = OUT_TM
    n_pairs = len(acts)
    in_specs = [pl.BlockSpec((tm, a.shape[1]), lambda i: (i, 0)) for a in acts]
    in_specs += [pl.BlockSpec(w.shape, lambda i: (0, 0)) for w in weights]
    in_specs += [
        pl.BlockSpec((tm, D_MODEL), lambda i: (i, 0)),
        pl.BlockSpec((1, D_MODEL), lambda i: (0, 0)),
        pl.BlockSpec((1, D_MODEL), lambda i: (0, 0)),
    ]
    return pl.pallas_call(
        functools.partial(_out_ln_kernel, n_pairs=n_pairs),
        grid=(TOKENS // tm,),
        in_specs=in_specs,
        out_specs=pl.BlockSpec((tm, D_MODEL), lambda i: (i, 0)),
        out_shape=jax.ShapeDtypeStruct((TOKENS, D_MODEL), jnp.float32),
        compiler_params=_cparams(("parallel",)),
        name=name,
    )(*acts, *weights, x, ln_g, ln_b)


def _split3(x):
    hi = x.astype(jnp.bfloat16)
    r1 = x - hi.astype(jnp.float32)
    mid = r1.astype(jnp.bfloat16)
    lo = (r1 - mid.astype(jnp.float32)).astype(jnp.bfloat16)
    return hi, mid, lo


def _moba_kernel(q_ref, k_ref, v_ref, g_ref, o_ref, kmean_ref):
    i = pl.program_id(2)
    L = MOBA_BLOCK

    @pl.when(i == 0)
    def _():
        means = [jnp.mean(k_ref[n * L: (n + 1) * L, :].astype(jnp.float32), axis=0, keepdims=True)
                 for n in range(MOBA_NB)]
        km = jnp.concatenate(means + [jnp.zeros((LANES - MOBA_NB, LANES), jnp.float32)], axis=0)
        for t, part in enumerate(_split3(km)):
            kmean_ref[t * LANES: (t + 1) * LANES, :] = part

    q = q_ref[...]
    lane = lax.broadcasted_iota(jnp.int32, (L, LANES), 1)
    second = (lane // (HALF // 2)) % 2
    row = lax.broadcasted_iota(jnp.int32, (L, L), 0)
    col = lax.broadcasted_iota(jnp.int32, (L, L), 1)
    d0 = pl.multiple_of(i * L, L)
    scale = MOBA_DH ** -0.5
    outs = []
    for e in range(2):
        qe = jnp.where(second == e, q, jnp.zeros_like(q))
        g3 = lax.dot_general(qe, kmean_ref[...], (((1,), (1,)), ((), ())), preferred_element_type=jnp.float32)
        gate = g3[:, :LANES] + g3[:, LANES: 2 * LANES] + g3[:, 2 * LANES:]
        gate = jnp.where(lane < i, gate, NEG)
        rank = jnp.zeros((L, LANES), jnp.float32)
        for n in range(MOBA_NB - 1):
            gn = jnp.sum(jnp.where(lane == n, gate, 0.0), axis=-1, keepdims=True)
            ahead = jnp.where(gn > gate, 1.0, jnp.where(gn == gate, jnp.where(lane > n, 1.0, 0.0), 0.0))
            rank = rank + ahead
        sel = jnp.where(lane < i, jnp.where(rank < float(MOBA_TOPK), 1.0, 0.0), 0.0)

        qs = (qe.astype(jnp.float32) * scale).astype(jnp.bfloat16)
        m0 = jnp.full((L, 1), NEG, jnp.float32)
        l0 = jnp.zeros((L, 1), jnp.float32)
        a0 = jnp.zeros((L, LANES), jnp.float32)
        carry = _softmax_block(qs, k_ref[pl.ds(d0, L), :], v_ref[pl.ds(d0, L), :], col <= row, m0, l0, a0)

        def body(j, carry):
            j0 = pl.multiple_of(j * L, L)
            keep = jnp.sum(jnp.where(lane == j, sel, 0.0), axis=-1, keepdims=True) > 0.0
            return _softmax_block(qs, k_ref[pl.ds(j0, L), :], v_ref[pl.ds(j0, L), :], keep, *carry)

        m, l, acc = lax.fori_loop(0, i, body, carry)
        outs.append(acc / l)
    o = jnp.where(lane < HALF, outs[0], outs[1])
    o_ref[...] = (o * _silu(g_ref[...].astype(jnp.float32))).astype(o_ref.dtype)


def _moba(h1):
    nq = SEQ // MOBA_BLOCK
    L = MOBA_BLOCK
    npair = MOBA_HEADS // 2
    return pl.pallas_call(
        _moba_kernel,
        grid=(BATCH, npair, nq),
        in_specs=[
            pl.BlockSpec((L, LANES), lambda b, p, i: (b * nq + i, p)),
            pl.BlockSpec((SEQ, LANES), lambda b, p, i: (b, npair + p)),
            pl.BlockSpec((SEQ, LANES), lambda b, p, i: (b, 2 * npair + p)),
            pl.BlockSpec((L, LANES), lambda b, p, i: (b * nq + i, 3 * npair + p)),
        ],
        out_specs=pl.BlockSpec((L, LANES), lambda b, p, i: (b * nq + i, p)),
        out_shape=jax.ShapeDtypeStruct((TOKENS, MOBA_W), jnp.bfloat16),
        scratch_shapes=[pltpu.VMEM((3 * LANES, LANES), jnp.bfloat16)],
        compiler_params=_cparams(("parallel", "parallel", "arbitrary")),
        name="moba_attention",
    )(h1, h1, h1, h1)


def _retnet_perm():
    per_head = np.concatenate([np.arange(0, RET_DK, 2), np.arange(1, RET_DK, 2)])
    return np.concatenate([h * RET_DK + per_head for h in range(RET_HEADS)])


def _pair_perm(n_heads, head_w, rope_off):
    q = MOBA_DH // 2
    cols = []
    for p in range(n_heads // 2):
        a = (2 * p) * head_w + rope_off
        b = (2 * p + 1) * head_w + rope_off
        cols += [np.arange(a, a + q), np.arange(b, b + q), np.arange(a + q, a + 2 * q), np.arange(b + q, b + 2 * q)]
    return np.concatenate(cols)


def kernel(x, positions, w_in_even, q_norm_even, w_uq_even, kv_norm_even, w_ukv_even, w_out_even, w_in_odd,
           w_out_odd, ln_g, ln_b):
    bf = jnp.bfloat16
    x0 = x.reshape(TOKENS, D_MODEL)
    pos = positions.reshape(TOKENS, 1)

    inv_r = (1.0 / (ROPE_THETA ** jnp.linspace(0.0, 1.0, RET_DK // 2, dtype=jnp.float32))).reshape(1, LANES)
    inv_rope = 1.0 / (ROPE_THETA ** (jnp.arange(0, MOBA_DH, 2, dtype=jnp.float32) / MOBA_DH))
    inv_p = jnp.tile(inv_rope, LANES // (MOBA_DH // 2)).reshape(1, LANES)
    cos_r, sin_r, cos_p, sin_p = _tables(pos, inv_r, inv_p)

    w_in = w_in_even[0]
    o = np.cumsum((0, 1024, 1024, RET_W, RET_W, MLA_Q_RANK, MLA_KV_RANK, MLA_ROPE, MLA_W))
    rq, rk, rv, rg, mq, mkv, mkr, mg = [w_in[:, o[t]: o[t + 1]] for t in range(8)]
    rperm = _retnet_perm()
    w_main = jnp.concatenate([rq[:, rperm], rk[:, rperm], rv, rg, mg], axis=1).astype(bf)
    q32 = MLA_ROPE // 2
    kr_dup = np.concatenate([np.arange(q32), np.arange(q32), np.arange(q32, 2 * q32), np.arange(q32, 2 * q32)])
    w_small = jnp.concatenate([mq, mkv, mkr[:, kr_dup]], axis=1).astype(bf)
    w_uq = w_uq_even[0]
    hw = MLA_NOPE + MLA_ROPE
    nope_cols = np.concatenate([np.arange(h * hw, h * hw + MLA_NOPE) for h in range(MLA_HEADS)])
    w_uq_p = jnp.concatenate([w_uq[:, nope_cols], w_uq[:, _pair_perm(MLA_HEADS, hw, MLA_NOPE)]], axis=1).astype(bf)
    w_ukv = w_ukv_even[0]
    kvw = MLA_NOPE + MLA_V
    knope_cols = np.concatenate([np.arange(h * kvw, h * kvw + MLA_NOPE) for h in range(MLA_HEADS)])
    v_cols = np.concatenate([np.arange(h * kvw + MLA_NOPE, (h + 1) * kvw) for h in range(MLA_HEADS)])
    w_ukv_p = jnp.concatenate([w_ukv[:, knope_cols], w_ukv[:, v_cols]], axis=1).astype(bf)
    w_out0 = w_out_even[0].astype(bf)

    h_main = _project(x0, w_main, cos_r, sin_r, mode="retnet", n_rot=2, name="proj_even")
    ret_out = _retention(h_main)
    q_full, k_full, v_m = _mla_prep(x0, w_small, q_norm_even[0].reshape(1, -1), kv_norm_even[0].reshape(1, -1),
                                    w_uq_p, w_ukv_p, cos_p, sin_p)
    mla_out = _mla_attention(q_full, k_full, v_m, h_main)
    x1 = _out_ln([ret_out, mla_out], [w_out0[:RET_W], w_out0[RET_W:]], x0,
                 ln_g[0].reshape(1, -1), ln_b[0].reshape(1, -1), name="out_ln_even")

    w_io = w_in_odd[0]
    pperm = _pair_perm(MOBA_HEADS, MOBA_DH, 0)
    w_odd = jnp.concatenate([w_io[:, :MOBA_W][:, pperm], w_io[:, MOBA_W: 2 * MOBA_W][:, pperm],
                             w_io[:, 2 * MOBA_W:]], axis=1).astype(bf)
    h1 = _project(x1, w_odd, cos_p, sin_p, mode="rope", n_rot=2, name="proj_odd")
    moba_out = _moba(h1)
    x2 = _out_ln([moba_out], [w_out_odd[0].astype(bf)], x1, ln_g[1].reshape(1, -1), ln_b[1].reshape(1, -1),
                 name="out_ln_odd")
    return x2.reshape(BATCH, SEQ, D_MODEL)
```

```python
import functools
import math

import numpy as np
import jax
import jax.numpy as jnp
from jax import lax
from jax.experimental import pallas as pl
from jax.experimental.pallas import tpu as pltpu

D_MODEL = 1024
BATCH = 8
SEQ = 2048
DEPTH = 2
TOKENS = BATCH * SEQ

RET_HEADS = 4
RET_DK = 256
RET_DV = 512
MLA_HEADS = 8
MLA_NOPE = 128
MLA_ROPE = 64
MLA_V = 128
MLA_Q_RANK = 256
MLA_KV_RANK = 256
MOBA_HEADS = 16
MOBA_DH = 64
MOBA_BLOCK = 256
MOBA_TOPK = 3
MOBA_NB = SEQ // MOBA_BLOCK

ROPE_THETA = 10000.0
LN_EPS = 1e-5
RMS_EPS = 1e-6
NEG = -1e30
DN_ALPHA = (2.0 * DEPTH) ** 0.25
LOG2E = math.log2(math.e)

RET_W = RET_HEADS * RET_DV
MLA_W = MLA_HEADS * MLA_V
MOBA_W = MOBA_HEADS * MOBA_DH

LANES = 128
HALF = LANES // 2
VMEM_LIMIT = 48 * 1024 * 1024

MAIN_RQ, MAIN_RK, MAIN_RV, MAIN_RG, MAIN_MG = 0, 1024, 2048, 4096, 6144
MAIN_W = 7168
SMALL_W = MLA_Q_RANK + MLA_KV_RANK + LANES

PROJ_TM = 1024
PROJ_TN = 1024
RET_CHUNK = 256
ATT_TQ = 256
ATT_TK = 256
PREP_TM = 512
OUT_TM = 512
TAB_TM = 1024


def _cparams(sem):
    return pltpu.CompilerParams(dimension_semantics=sem, vmem_limit_bytes=VMEM_LIMIT)


def _silu(g):
    return g / (1.0 + jnp.exp(-g))


def _tables_kernel(pos_ref, inv_r_ref, inv_p_ref, cr_ref, sr_ref, cp_ref, sp_ref):
    pos = pos_ref[...].astype(jnp.float32)
    ang_r = pos * inv_r_ref[...]
    cr_ref[...] = jnp.cos(ang_r)
    sr_ref[...] = jnp.sin(ang_r)
    ang_p = pos * inv_p_ref[...]
    lane = lax.broadcasted_iota(jnp.int32, ang_p.shape, 1)
    cp_ref[...] = jnp.cos(ang_p)
    sin_p = jnp.sin(ang_p)
    sp_ref[...] = jnp.where(lane < HALF, -sin_p, sin_p)


def _tables(pos, inv_r, inv_p):
    tab = jax.ShapeDtypeStruct((TOKENS, LANES), jnp.float32)
    row = pl.BlockSpec((TAB_TM, LANES), lambda i: (i, 0))
    cst = pl.BlockSpec((1, LANES), lambda i: (0, 0))
    return pl.pallas_call(
        _tables_kernel,
        grid=(TOKENS // TAB_TM,),
        in_specs=[pl.BlockSpec((TAB_TM, 1), lambda i: (i, 0)), cst, cst],
        out_specs=[row, row, row, row],
        out_shape=[tab, tab, tab, tab],
        compiler_params=_cparams(("parallel",)),
        name="rot_tables",
    )(pos, inv_r, inv_p)


def _proj_kernel(x_ref, w_ref, cos_ref, sin_ref, o_ref, xb_ref, *, mode, n_rot):
    j = pl.program_id(1)

    @pl.when(j == 0)
    def _():
        xb_ref[...] = x_ref[...].astype(jnp.bfloat16)

    acc = jnp.dot(xb_ref[...], w_ref[...], preferred_element_type=jnp.float32)

    @pl.when(j >= n_rot)
    def _():
        o_ref[...] = acc.astype(o_ref.dtype)

    @pl.when(j < n_rot)
    def _():
        cos = cos_ref[...]
        sin = sin_ref[...]
        if mode == "retnet":
            scale = jnp.where(j == 1, RET_DK ** -0.5, 1.0).astype(jnp.float32)
            for h in range(PROJ_TN // RET_DK):
                x1 = acc[:, h * RET_DK: h * RET_DK + LANES]
                x2 = acc[:, h * RET_DK + LANES: (h + 1) * RET_DK]
                o_ref[:, h * RET_DK: h * RET_DK + LANES] = ((x1 * cos - x2 * sin) * scale).astype(o_ref.dtype)
                o_ref[:, h * RET_DK + LANES: (h + 1) * RET_DK] = ((x2 * cos + x1 * sin) * scale).astype(o_ref.dtype)
        else:
            scale = jnp.where(j == 0, MOBA_DH ** -0.5 * LOG2E, 1.0).astype(jnp.float32)
            for g in range(PROJ_TN // LANES):
                xg = acc[:, g * LANES: (g + 1) * LANES]
                rot = (xg * cos + pltpu.roll(xg, HALF, axis=1) * sin) * scale
                o_ref[:, g * LANES: (g + 1) * LANES] = rot.astype(o_ref.dtype)


def _project(x, w, cos, sin, *, mode, n_rot, name):
    n = w.shape[1]
    return pl.pallas_call(
        functools.partial(_proj_kernel, mode=mode, n_rot=n_rot),
        grid=(TOKENS // PROJ_TM, n // PROJ_TN),
        in_specs=[
            pl.BlockSpec((PROJ_TM, D_MODEL), lambda i, j: (i, 0)),
            pl.BlockSpec((D_MODEL, PROJ_TN), lambda i, j: (0, j)),
            pl.BlockSpec((PROJ_TM, LANES), lambda i, j: (i, 0)),
            pl.BlockSpec((PROJ_TM, LANES), lambda i, j: (i, 0)),
        ],
        out_specs=pl.BlockSpec((PROJ_TM, PROJ_TN), lambda i, j: (i, j)),
        out_shape=jax.ShapeDtypeStruct((TOKENS, n), jnp.bfloat16),
        scratch_shapes=[pltpu.VMEM((PROJ_TM, D_MODEL), jnp.bfloat16)],
        compiler_params=_cparams(("parallel", "arbitrary")),
        name=name,
    )(x, w, cos, sin)


def _proj_t_kernel(x_ref, wt_ref, o_ref, xb_ref):
    @pl.when(pl.program_id(1) == 0)
    def _():
        xb_ref[...] = x_ref[...].astype(jnp.bfloat16)

    o_ref[...] = lax.dot_general(wt_ref[...], xb_ref[...], (((1,), (1,)), ((), ())),
                                 preferred_element_type=jnp.float32).astype(o_ref.dtype)


def _project_t(x, wt, name):
    n = wt.shape[0]
    return pl.pallas_call(
        _proj_t_kernel,
        grid=(TOKENS // PROJ_TM, n // PROJ_TN),
        in_specs=[
            pl.BlockSpec((PROJ_TM, D_MODEL), lambda i, j: (i, 0)),
            pl.BlockSpec((PROJ_TN, D_MODEL), lambda i, j: (j, 0)),
        ],
        out_specs=pl.BlockSpec((PROJ_TN, PROJ_TM), lambda i, j: (j, i)),
        out_shape=jax.ShapeDtypeStruct((n, TOKENS), jnp.bfloat16),
        scratch_shapes=[pltpu.VMEM((PROJ_TM, D_MODEL), jnp.bfloat16)],
        compiler_params=_cparams(("parallel", "arbitrary")),
        name=name,
    )(x, wt)


def _retention_kernel(q_ref, k_ref, v_ref, g_ref, o_ref, state_ref):
    c = pl.program_id(2)
    C = RET_CHUNK

    @pl.when(c == 0)
    def _():
        state_ref[...] = jnp.zeros_like(state_ref)

    def log_gamma(shape):
        hf = jnp.full(shape, pl.program_id(1), jnp.int32).astype(jnp.float32)
        return jnp.log(1.0 - jnp.exp2(-5.0 - hf))

    q = q_ref[...]
    k = k_ref[...]
    v = v_ref[...]

    row = lax.broadcasted_iota(jnp.int32, (C, C), 0)
    col = lax.broadcasted_iota(jnp.int32, (C, C), 1)
    rel = (row - col).astype(jnp.float32)
    decay = jnp.where(rel >= 0, jnp.exp(log_gamma((C, C)) * jnp.maximum(rel, 0.0)), 0.0)

    s = lax.dot_general(q, k, (((1,), (1,)), ((), ())), preferred_element_type=jnp.float32)
    inner = (s * decay).astype(jnp.bfloat16)
    o = jnp.dot(inner, v, preferred_element_type=jnp.float32)

    state = state_ref[...]
    idx_v = lax.broadcasted_iota(jnp.int32, (C, RET_DV), 0).astype(jnp.float32)
    xi = jnp.exp(log_gamma((C, RET_DV)) * (idx_v + 1.0))
    o = o + jnp.dot(q, state.astype(jnp.bfloat16), preferred_element_type=jnp.float32) * xi

    idx_k = lax.broadcasted_iota(jnp.int32, (C, RET_DK), 0).astype(jnp.float32)
    zeta = jnp.exp(log_gamma((C, RET_DK)) * (C - 1.0 - idx_k))
    kz = (k.astype(jnp.float32) * zeta).astype(jnp.bfloat16)
    kv = lax.dot_general(kz, v, (((0,), (0,)), ((), ())), preferred_element_type=jnp.float32)
    g_chunk = jnp.exp(log_gamma((RET_DK, RET_DV)) * float(C))
    state_ref[...] = state * g_chunk + kv

    mu = jnp.mean(o, axis=-1, keepdims=True)
    oc = o - mu
    var = jnp.mean(oc * oc, axis=-1, keepdims=True)
    on = oc * lax.rsqrt(var + LN_EPS)
    o_ref[...] = (on * _silu(g_ref[...].astype(jnp.float32))).astype(o_ref.dtype)


def _retention(h_main):
    nc = SEQ // RET_CHUNK
    C = RET_CHUNK
    return pl.pallas_call(
        _retention_kernel,
        grid=(BATCH, RET_HEADS, nc),
        in_specs=[
            pl.BlockSpec((C, RET_DK), lambda b, h, c: (b * nc + c, MAIN_RQ // RET_DK + h)),
            pl.BlockSpec((C, RET_DK), lambda b, h, c: (b * nc + c, MAIN_RK // RET_DK + h)),
            pl.BlockSpec((C, RET_DV), lambda b, h, c: (b * nc + c, MAIN_RV // RET_DV + h)),
            pl.BlockSpec((C, RET_DV), lambda b, h, c: (b * nc + c, MAIN_RG // RET_DV + h)),
        ],
        out_specs=pl.BlockSpec((C, RET_DV), lambda b, h, c: (b * nc + c, h)),
        out_shape=jax.ShapeDtypeStruct((TOKENS, RET_W), jnp.bfloat16),
        scratch_shapes=[pltpu.VMEM((RET_DK, RET_DV), jnp.float32)],
        compiler_params=_cparams(("parallel", "parallel", "arbitrary")),
        name="retention",
    )(h_main, h_main, h_main, h_main)


def _rms_norm(x, g):
    return x * lax.rsqrt(jnp.mean(x * x, axis=-1, keepdims=True) + RMS_EPS) * g


def _mla_prep_kernel(x_ref, ws_ref, qg_ref, kvg_ref, wuq_ref, wukv_ref, cos_ref, sin_ref,
                     q_ref, k_ref, v_ref):
    xb = x_ref[...].astype(jnp.bfloat16)
    hs = jnp.dot(xb, ws_ref[...], preferred_element_type=jnp.float32)
    mq = hs[:, :MLA_Q_RANK]
    mkv = hs[:, MLA_Q_RANK: MLA_Q_RANK + MLA_KV_RANK]
    mkr = hs[:, MLA_Q_RANK + MLA_KV_RANK:]
    cos = cos_ref[...]
    sin = sin_ref[...]

    qn = _rms_norm(mq, qg_ref[...]).astype(jnp.bfloat16)
    kvn = _rms_norm(mkv, kvg_ref[...]).astype(jnp.bfloat16)
    qm = jnp.dot(qn, wuq_ref[...], preferred_element_type=jnp.float32)
    kv = jnp.dot(kvn, wukv_ref[...], preferred_element_type=jnp.float32)

    kpe = (mkr * cos + pltpu.roll(mkr, HALF, axis=1) * sin).astype(k_ref.dtype)
    scale = (MLA_NOPE + MLA_ROPE) ** -0.5 * LOG2E
    lane = lax.broadcasted_iota(jnp.int32, (x_ref.shape[0], LANES), 1)
    second = (lane // (HALF // 2)) % 2
    nope_w = MLA_HEADS * MLA_NOPE
    for p in range(MLA_HEADS // 2):
        pe = qm[:, nope_w + p * LANES: nope_w + (p + 1) * LANES]
        pe = (pe * cos + pltpu.roll(pe, HALF, axis=1) * sin) * scale
        for e in range(2):
            h = 2 * p + e
            base = h * 2 * LANES
            q_ref[:, base: base + LANES] = (qm[:, h * LANES: (h + 1) * LANES] * scale).astype(q_ref.dtype)
            q_ref[:, base + LANES: base + 2 * LANES] = jnp.where(second == e, pe, 0.0).astype(q_ref.dtype)
            k_ref[:, base: base + LANES] = kv[:, h * LANES: (h + 1) * LANES].astype(k_ref.dtype)
            k_ref[:, base + LANES: base + 2 * LANES] = kpe
    v_ref[...] = kv[:, nope_w:].astype(v_ref.dtype)


def _mla_prep(x, w_small, q_g, kv_g, w_uq, w_ukv, cos_p, sin_p):
    tm = PREP_TM
    full = lambda a: pl.BlockSpec(a.shape, lambda i: (0, 0))
    qk_w = MLA_HEADS * 2 * LANES
    return pl.pallas_call(
        _mla_prep_kernel,
        grid=(TOKENS // tm,),
        in_specs=[
            pl.BlockSpec((tm, D_MODEL), lambda i: (i, 0)),
            full(w_small), full(q_g), full(kv_g), full(w_uq), full(w_ukv),
            pl.BlockSpec((tm, LANES), lambda i: (i, 0)),
            pl.BlockSpec((tm, LANES), lambda i: (i, 0)),
        ],
        out_specs=[
            pl.BlockSpec((tm, qk_w), lambda i: (i, 0)),
            pl.BlockSpec((tm, qk_w), lambda i: (i, 0)),
            pl.BlockSpec((tm, MLA_W), lambda i: (i, 0)),
        ],
        out_shape=[
            jax.ShapeDtypeStruct((TOKENS, qk_w), jnp.bfloat16),
            jax.ShapeDtypeStruct((TOKENS, qk_w), jnp.bfloat16),
            jax.ShapeDtypeStruct((TOKENS, MLA_W), jnp.bfloat16),
        ],
        compiler_params=_cparams(("parallel",)),
        name="mla_prep",
    )(x, w_small, q_g, kv_g, w_uq, w_ukv, cos_p, sin_p)


def _mla_attn_kernel(q_ref, k_ref, v_ref, g_ref, o_ref):
    row = lax.broadcasted_iota(jnp.int32, (ATT_TQ, ATT_TQ), 0)
    col = lax.broadcasted_iota(jnp.int32, (ATT_TQ, ATT_TQ), 1)
    for i in range(SEQ // ATT_TQ):
        r0 = i * ATT_TQ
        n = r0 + ATT_TQ
        q = q_ref[r0:n, :]
        s = lax.dot_general(q, k_ref[0:n, :], (((1,), (1,)), ((), ())), preferred_element_type=jnp.float32)
        diag = jnp.where(col <= row, s[:, r0:], NEG)
        s = diag if i == 0 else jnp.concatenate([s[:, :r0], diag], axis=1)
        m = jnp.max(s, axis=-1, keepdims=True)
        p = jnp.exp2(s - m)
        l = jnp.sum(p, axis=-1, keepdims=True)
        acc = jnp.dot(p.astype(jnp.bfloat16), v_ref[0:n, :], preferred_element_type=jnp.float32)
        gate = _silu(g_ref[r0:n, :].astype(jnp.float32))
        o_ref[r0:n, :] = (acc / l * gate).astype(o_ref.dtype)


def _mla_attention(q_full, k_full, v_m, h_main):
    qk_d = 2 * LANES
    return pl.pallas_call(
        _mla_attn_kernel,
        grid=(BATCH, MLA_HEADS),
        in_specs=[
            pl.BlockSpec((SEQ, qk_d), lambda b, h: (b, h)),
            pl.BlockSpec((SEQ, qk_d), lambda b, h: (b, h)),
            pl.BlockSpec((SEQ, MLA_V), lambda b, h: (b, h)),
            pl.BlockSpec((SEQ, MLA_V), lambda b, h: (b, MAIN_MG // MLA_V + h)),
        ],
        out_specs=pl.BlockSpec((SEQ, MLA_V), lambda b, h: (b, h)),
        out_shape=jax.ShapeDtypeStruct((TOKENS, MLA_W), jnp.bfloat16),
        compiler_params=_cparams(("parallel", "parallel")),
        name="mla_attention",
    )(q_full, k_full, v_m, h_main)


def _out_ln_kernel(*refs, feature_major):
    n_pairs = len(feature_major)
    a_refs = refs[:n_pairs]
    w_refs = refs[n_pairs: 2 * n_pairs]
    x_ref, g_ref, b_ref, o_ref = refs[2 * n_pairs:]
    y = None
    for a_ref, w_ref, fm in zip(a_refs, w_refs, feature_major):
        dims = (((0,), (0,)), ((), ())) if fm else (((1,), (0,)), ((), ()))
        t = lax.dot_general(a_ref[...], w_ref[...], dims, preferred_element_type=jnp.float32)
        y = t if y is None else y + t
    z = DN_ALPHA * x_ref[...] + y
    mu = jnp.mean(z, axis=-1, keepdims=True)
    zc = z - mu
    var = jnp.mean(zc * zc, axis=-1, keepdims=True)
    o_ref[...] = zc * lax.rsqrt(var + LN_EPS) * g_ref[...] + b_ref[...]


def _out_ln(acts, weights, x, ln_g, ln_b, name, feature_major=None):
    tm = OUT_TM
    feature_major = tuple(feature_major or (False,) * len(acts))
    in_specs = [pl.BlockSpec((a.shape[0], tm), lambda i: (0, i)) if fm else
                pl.BlockSpec((tm, a.shape[1]), lambda i: (i, 0)) for a, fm in zip(acts, feature_major)]
    in_specs += [pl.BlockSpec(w.shape, lambda i: (0, 0)) for w in weights]
    in_specs += [
        pl.BlockSpec((tm, D_MODEL), lambda i: (i, 0)),
        pl.BlockSpec((1, D_MODEL), lambda i: (0, 0)),
        pl.BlockSpec((1, D_MODEL), lambda i: (0, 0)),
    ]
    return pl.pallas_call(
        functools.partial(_out_ln_kernel, feature_major=feature_major),
        grid=(TOKENS // tm,),
        in_specs=in_specs,
        out_specs=pl.BlockSpec((tm, D_MODEL), lambda i: (i, 0)),
        out_shape=jax.ShapeDtypeStruct((TOKENS, D_MODEL), jnp.float32),
        compiler_params=_cparams(("parallel",)),
        name=name,
    )(*acts, *weights, x, ln_g, ln_b)


def _split3(x):
    hi = x.astype(jnp.bfloat16)
    r1 = x - hi.astype(jnp.float32)
    mid = r1.astype(jnp.bfloat16)
    lo = (r1 - mid.astype(jnp.float32)).astype(jnp.bfloat16)
    return hi, mid, lo


def _moba_kernel(q_ref, k_ref, vt_ref, gt_ref, o_ref):
    L = MOBA_BLOCK
    NB = MOBA_NB
    means = [jnp.mean(k_ref[n * L: (n + 1) * L, :].astype(jnp.float32), axis=0, keepdims=True) for n in range(NB)]
    km = jnp.concatenate(means, axis=0)
    parts = [p.astype(jnp.float32) for p in _split3(km)]
    km3 = jnp.concatenate(parts + [jnp.zeros_like(km)], axis=0).astype(jnp.bfloat16)

    q = q_ref[...]
    lane = lax.broadcasted_iota(jnp.int32, (SEQ, LANES), 1)
    second = (lane // (HALF // 2)) % 2
    blk = lax.broadcasted_iota(jnp.int32, (NB, SEQ), 0)
    qblk = lax.broadcasted_iota(jnp.int32, (NB, SEQ), 1) // L
    past = blk < qblk
    krow = lax.broadcasted_iota(jnp.int32, (L, L), 0)
    qcol = lax.broadcasted_iota(jnp.int32, (L, L), 1)
    hd = MOBA_DH
    for e in range(2):
        qe = jnp.where(second == e, q, jnp.zeros_like(q))
        g3 = lax.dot_general(km3, qe, (((1,), (1,)), ((), ())), preferred_element_type=jnp.float32)
        gate = jnp.where(past, g3[0:NB] + g3[NB: 2 * NB] + g3[2 * NB: 3 * NB], NEG)
        rank = jnp.zeros((NB, SEQ), jnp.float32)
        for n in range(NB - 1):
            gn = gate[n: n + 1, :]
            rank = rank + jnp.where(gn > gate, 1.0, jnp.where(gn == gate, jnp.where(blk > n, 1.0, 0.0), 0.0))
        sel = jnp.where(past, jnp.where(rank < float(MOBA_TOPK), 1.0, 0.0), 0.0)

        for i in range(NB):
            r0 = i * L
            n_keys = r0 + L
            st = lax.dot_general(k_ref[0:n_keys, :], qe[r0:n_keys, :], (((1,), (1,)), ((), ())),
                                 preferred_element_type=jnp.float32)
            pieces = [jnp.where(sel[j: j + 1, r0:n_keys] > 0.0, st[j * L: (j + 1) * L, :], NEG) for j in range(i)]
            pieces.append(jnp.where(krow <= qcol, st[r0:, :], NEG))
            sm = pieces[0] if i == 0 else jnp.concatenate(pieces, axis=0)
            m = jnp.max(sm, axis=0, keepdims=True)
            p = jnp.exp2(sm - m)
            l = jnp.sum(p, axis=0, keepdims=True)
            ot = jnp.dot(vt_ref[e * hd: (e + 1) * hd, 0:n_keys], p.astype(jnp.bfloat16),
                         preferred_element_type=jnp.float32)
            gate_t = _silu(gt_ref[e * hd: (e + 1) * hd, r0:n_keys].astype(jnp.float32))
            o_ref[e * hd: (e + 1) * hd, r0:n_keys] = (ot / l * gate_t).astype(o_ref.dtype)


def _moba(h_qk, h_vg_t):
    npair = MOBA_HEADS // 2
    return pl.pallas_call(
        _moba_kernel,
        grid=(BATCH, npair),
        in_specs=[
            pl.BlockSpec((SEQ, LANES), lambda b, p: (b, p)),
            pl.BlockSpec((SEQ, LANES), lambda b, p: (b, npair + p)),
            pl.BlockSpec((LANES, SEQ), lambda b, p: (p, b)),
            pl.BlockSpec((LANES, SEQ), lambda b, p: (npair + p, b)),
        ],
        out_specs=pl.BlockSpec((LANES, SEQ), lambda b, p: (p, b)),
        out_shape=jax.ShapeDtypeStruct((MOBA_W, TOKENS), jnp.bfloat16),
        compiler_params=_cparams(("parallel", "parallel")),
        name="moba_attention",
    )(h_qk, h_qk, h_vg_t, h_vg_t)


def _retnet_perm():
    per_head = np.concatenate([np.arange(0, RET_DK, 2), np.arange(1, RET_DK, 2)])
    return np.concatenate([h * RET_DK + per_head for h in range(RET_HEADS)])


def _pair_perm(n_heads, head_w, rope_off):
    q = MOBA_DH // 2
    cols = []
    for p in range(n_heads // 2):
        a = (2 * p) * head_w + rope_off
        b = (2 * p + 1) * head_w + rope_off
        cols += [np.arange(a, a + q), np.arange(b, b + q), np.arange(a + q, a + 2 * q), np.arange(b + q, b + 2 * q)]
    return np.concatenate(cols)


def kernel(x, positions, w_in_even, q_norm_even, w_uq_even, kv_norm_even, w_ukv_even, w_out_even, w_in_odd,
           w_out_odd, ln_g, ln_b):
    bf = jnp.bfloat16
    x0 = x.reshape(TOKENS, D_MODEL)
    pos = positions.reshape(TOKENS, 1)

    inv_r = (1.0 / (ROPE_THETA ** jnp.linspace(0.0, 1.0, RET_DK // 2, dtype=jnp.float32))).reshape(1, LANES)
    inv_rope = 1.0 / (ROPE_THETA ** (jnp.arange(0, MOBA_DH, 2, dtype=jnp.float32) / MOBA_DH))
    inv_p = jnp.tile(inv_rope, LANES // (MOBA_DH // 2)).reshape(1, LANES)
    cos_r, sin_r, cos_p, sin_p = _tables(pos, inv_r, inv_p)

    w_in = w_in_even[0]
    o = np.cumsum((0, 1024, 1024, RET_W, RET_W, MLA_Q_RANK, MLA_KV_RANK, MLA_ROPE, MLA_W))
    rq, rk, rv, rg, mq, mkv, mkr, mg = [w_in[:, o[t]: o[t + 1]] for t in range(8)]
    rperm = _retnet_perm()
    w_main = jnp.concatenate([rq[:, rperm], rk[:, rperm], rv, rg, mg], axis=1).astype(bf)
    q32 = MLA_ROPE // 2
    kr_dup = np.concatenate([np.arange(q32), np.arange(q32), np.arange(q32, 2 * q32), np.arange(q32, 2 * q32)])
    w_small = jnp.concatenate([mq, mkv, mkr[:, kr_dup]], axis=1).astype(bf)
    w_uq = w_uq_even[0]
    hw = MLA_NOPE + MLA_ROPE
    nope_cols = np.concatenate([np.arange(h * hw, h * hw + MLA_NOPE) for h in range(MLA_HEADS)])
    w_uq_p = jnp.concatenate([w_uq[:, nope_cols], w_uq[:, _pair_perm(MLA_HEADS, hw, MLA_NOPE)]], axis=1).astype(bf)
    w_ukv = w_ukv_even[0]
    kvw = MLA_NOPE + MLA_V
    knope_cols = np.concatenate([np.arange(h * kvw, h * kvw + MLA_NOPE) for h in range(MLA_HEADS)])
    v_cols = np.concatenate([np.arange(h * kvw + MLA_NOPE, (h + 1) * kvw) for h in range(MLA_HEADS)])
    w_ukv_p = jnp.concatenate([w_ukv[:, knope_cols], w_ukv[:, v_cols]], axis=1).astype(bf)
    w_out0 = w_out_even[0].astype(bf)

    h_main = _project(x0, w_main, cos_r, sin_r, mode="retnet", n_rot=2, name="proj_even")
    ret_out = _retention(h_main)
    q_full, k_full, v_m = _mla_prep(x0, w_small, q_norm_even[0].reshape(1, -1), kv_norm_even[0].reshape(1, -1),
                                    w_uq_p, w_ukv_p, cos_p, sin_p)
    mla_out = _mla_attention(q_full, k_full, v_m, h_main)
    x1 = _out_ln([ret_out, mla_out], [w_out0[:RET_W], w_out0[RET_W:]], x0,
                 ln_g[0].reshape(1, -1), ln_b[0].reshape(1, -1), name="out_ln_even")

    w_io = w_in_odd[0]
    pperm = _pair_perm(MOBA_HEADS, MOBA_DH, 0)
    w_qk = jnp.concatenate([w_io[:, :MOBA_W][:, pperm], w_io[:, MOBA_W: 2 * MOBA_W][:, pperm]], axis=1).astype(bf)
    w_vg_t = w_io[:, 2 * MOBA_W:].T.astype(bf)
    h_qk = _project(x1, w_qk, cos_p, sin_p, mode="rope", n_rot=2, name="proj_odd_qk")
    h_vg_t = _project_t(x1, w_vg_t, name="proj_odd_vg")
    moba_out_t = _moba(h_qk, h_vg_t)
    x2 = _out_ln([moba_out_t], [w_out_odd[0].astype(bf)], x1, ln_g[1].reshape(1, -1), ln_b[1].reshape(1, -1),
                 name="out_ln_odd", feature_major=(True,))
    return x2.reshape(BATCH, SEQ, D_MODEL)
```

```python
import functools
import math

import numpy as np
import jax
import jax.numpy as jnp
from jax import lax
from jax.experimental import pallas as pl
from jax.experimental.pallas import tpu as pltpu

D_MODEL = 1024
BATCH = 8
SEQ = 2048
DEPTH = 2
TOKENS = BATCH * SEQ

RET_HEADS = 4
RET_DK = 256
RET_DV = 512
MLA_HEADS = 8
MLA_NOPE = 128
MLA_ROPE = 64
MLA_V = 128
MLA_Q_RANK = 256
MLA_KV_RANK = 256
MOBA_HEADS = 16
MOBA_DH = 64
MOBA_BLOCK = 256
MOBA_TOPK = 3
MOBA_NB = SEQ // MOBA_BLOCK
MOBA_LOOKAHEAD = 2
ATT_LOOKAHEAD = 2

ROPE_THETA = 10000.0
LN_EPS = 1e-5
RMS_EPS = 1e-6
NEG = -1e30
DN_ALPHA = (2.0 * DEPTH) ** 0.25
LOG2E = math.log2(math.e)

RET_W = RET_HEADS * RET_DV
MLA_W = MLA_HEADS * MLA_V
MOBA_W = MOBA_HEADS * MOBA_DH

LANES = 128
HALF = LANES // 2
BF16_SUBLANES = 16
VMEM_LIMIT = 48 * 1024 * 1024

MAIN_RQ, MAIN_RK, MAIN_RV, MAIN_RG, MAIN_MG = 0, 1024, 2048, 4096, 6144
MAIN_W = 7168
SMALL_W = MLA_Q_RANK + MLA_KV_RANK + LANES

PROJ_TM = 512
PROJ_TN = 1024
OUT_SUB = 256
RET_CHUNK = 256
RET_CHUNKS_PER_STEP = 2
ATT_TQ = 256
ATT_TK = 256
PREP_TM = 512
OUT_TM = 512
TAB_TM = 1024


def _cparams(sem):
    return pltpu.CompilerParams(dimension_semantics=sem, vmem_limit_bytes=VMEM_LIMIT)


def _silu(g):
    return g / (1.0 + jnp.exp(-g))


def _tables_kernel(pos_ref, inv_r_ref, inv_p_ref, cr_ref, sr_ref, cp_ref, sp_ref):
    pos = pos_ref[...].astype(jnp.float32)
    ang_r = pos * inv_r_ref[...]
    cr_ref[...] = jnp.cos(ang_r)
    sr_ref[...] = jnp.sin(ang_r)
    ang_p = pos * inv_p_ref[...]
    lane = lax.broadcasted_iota(jnp.int32, ang_p.shape, 1)
    cp_ref[...] = jnp.cos(ang_p)
    sin_p = jnp.sin(ang_p)
    sp_ref[...] = jnp.where(lane < HALF, -sin_p, sin_p)


def _tables(pos, inv_r, inv_p):
    tab = jax.ShapeDtypeStruct((TOKENS, LANES), jnp.float32)
    row = pl.BlockSpec((TAB_TM, LANES), lambda i: (i, 0))
    cst = pl.BlockSpec((1, LANES), lambda i: (0, 0))
    return pl.pallas_call(
        _tables_kernel,
        grid=(TOKENS // TAB_TM,),
        in_specs=[pl.BlockSpec((TAB_TM, 1), lambda i: (i, 0)), cst, cst],
        out_specs=[row, row, row, row],
        out_shape=[tab, tab, tab, tab],
        compiler_params=_cparams(("parallel",)),
        name="rot_tables",
    )(pos, inv_r, inv_p)


def _resident(shape):
    return pl.BlockSpec(shape, lambda i: (0,) * len(shape), pipeline_mode=pl.Buffered(1))


def _proj_even_kernel(x_ref, w_ref, cos_ref, sin_ref, o_ref):
    xb = x_ref[...].astype(jnp.bfloat16)
    cos = cos_ref[...]
    sin = sin_ref[...]
    for c0 in range(0, MAIN_W, PROJ_TN):
        acc = jnp.dot(xb, w_ref[:, c0: c0 + PROJ_TN], preferred_element_type=jnp.float32)
        if c0 not in (MAIN_RQ, MAIN_RK):
            o_ref[:, c0: c0 + PROJ_TN] = acc.astype(o_ref.dtype)
            continue
        for h0 in range(0, PROJ_TN, RET_DK):
            x1 = acc[:, h0: h0 + LANES]
            x2 = acc[:, h0 + LANES: h0 + RET_DK]
            r1 = x1 * cos - x2 * sin
            r2 = x2 * cos + x1 * sin
            if c0 == MAIN_RK:
                r1 = r1 * RET_DK ** -0.5
                r2 = r2 * RET_DK ** -0.5
            o_ref[:, c0 + h0: c0 + h0 + LANES] = r1.astype(o_ref.dtype)
            o_ref[:, c0 + h0 + LANES: c0 + h0 + RET_DK] = r2.astype(o_ref.dtype)


def _project_even(x, w, cos, sin):
    tm = PROJ_TM
    row = lambda width: pl.BlockSpec((tm, width), lambda i: (i, 0))
    return pl.pallas_call(
        _proj_even_kernel,
        grid=(TOKENS // tm,),
        in_specs=[row(D_MODEL), _resident(w.shape), row(LANES), row(LANES)],
        out_specs=row(MAIN_W),
        out_shape=jax.ShapeDtypeStruct((TOKENS, MAIN_W), jnp.bfloat16),
        compiler_params=_cparams(("parallel",)),
        name="proj_even",
    )(x, w, cos, sin)


def _proj_odd_kernel(x_ref, wqk_ref, wvgt_ref, cos_ref, sin_ref, oqk_ref, ovgt_ref):
    xb = x_ref[...].astype(jnp.bfloat16)
    cos = cos_ref[...]
    sin = sin_ref[...]
    for c0 in range(0, 2 * MOBA_W, PROJ_TN):
        acc = jnp.dot(xb, wqk_ref[:, c0: c0 + PROJ_TN], preferred_element_type=jnp.float32)
        for g0 in range(0, PROJ_TN, LANES):
            xg = acc[:, g0: g0 + LANES]
            rot = xg * cos + pltpu.roll(xg, HALF, axis=1) * sin
            if c0 < MOBA_W:
                rot = rot * (MOBA_DH ** -0.5 * LOG2E)
            oqk_ref[:, c0 + g0: c0 + g0 + LANES] = rot.astype(oqk_ref.dtype)
    for r0 in range(0, 2 * MOBA_W, PROJ_TN):
        ovgt_ref[r0: r0 + PROJ_TN, :] = lax.dot_general(
            wvgt_ref[r0: r0 + PROJ_TN, :], xb, (((1,), (1,)), ((), ())),
            preferred_element_type=jnp.float32).astype(ovgt_ref.dtype)


def _project_odd(x, w_qk, w_vg_t, cos, sin):
    tm = PROJ_TM
    row = lambda width: pl.BlockSpec((tm, width), lambda i: (i, 0))
    return pl.pallas_call(
        _proj_odd_kernel,
        grid=(TOKENS // tm,),
        in_specs=[row(D_MODEL), _resident(w_qk.shape), _resident(w_vg_t.shape), row(LANES), row(LANES)],
        out_specs=[row(2 * MOBA_W), pl.BlockSpec((2 * MOBA_W, tm), lambda i: (0, i))],
        out_shape=[jax.ShapeDtypeStruct((TOKENS, 2 * MOBA_W), jnp.bfloat16),
                   jax.ShapeDtypeStruct((2 * MOBA_W, TOKENS), jnp.bfloat16)],
        compiler_params=_cparams(("parallel",)),
        name="proj_odd",
    )(x, w_qk, w_vg_t, cos, sin)


def _retention_kernel(q_ref, k_ref, v_ref, g_ref, o_ref, state_ref, decay_ref, xi_ref, zeta_ref, gch_ref):
    C = RET_CHUNK

    @pl.when((pl.program_id(1) == 0) & (pl.program_id(2) == 0))
    def _():
        def log_gamma(shape):
            hf = jnp.full(shape, pl.program_id(0), jnp.int32).astype(jnp.float32)
            return jnp.log(1.0 - jnp.exp2(-5.0 - hf))

        row = lax.broadcasted_iota(jnp.int32, (C, C), 0)
        col = lax.broadcasted_iota(jnp.int32, (C, C), 1)
        rel = (row - col).astype(jnp.float32)
        decay_ref[...] = jnp.where(rel >= 0, jnp.exp(log_gamma((C, C)) * jnp.maximum(rel, 0.0)), 0.0)
        idx = lax.broadcasted_iota(jnp.int32, (C, LANES), 0).astype(jnp.float32)
        xi_ref[...] = jnp.exp(log_gamma((C, LANES)) * (idx + 1.0))
        zeta_ref[...] = jnp.exp(log_gamma((C, LANES)) * (C - 1.0 - idx))
        gch_ref[...] = jnp.exp(log_gamma(gch_ref.shape) * float(C))

    @pl.when(pl.program_id(2) == 0)
    def _():
        state_ref[...] = jnp.zeros_like(state_ref)

    decay = decay_ref[...]
    xi = jnp.concatenate([xi_ref[...]] * (RET_DV // LANES), axis=1)
    zeta = jnp.concatenate([zeta_ref[...]] * (RET_DK // LANES), axis=1)
    g_chunk = gch_ref[0:1, :]

    outs = []
    for t in range(RET_CHUNKS_PER_STEP):
        rows = slice(t * C, (t + 1) * C)
        q = q_ref[rows, :]
        k = k_ref[rows, :]
        v = v_ref[rows, :]
        s = lax.dot_general(q, k, (((1,), (1,)), ((), ())), preferred_element_type=jnp.float32)
        state = state_ref[...]
        cross = jnp.dot(q, state.astype(jnp.bfloat16), preferred_element_type=jnp.float32)
        kz = (k.astype(jnp.float32) * zeta).astype(jnp.bfloat16)
        kv = lax.dot_general(kz, v, (((0,), (0,)), ((), ())), preferred_element_type=jnp.float32)
        state_ref[...] = state * g_chunk + kv
        inner = (s * decay).astype(jnp.bfloat16)
        outs.append(jnp.dot(inner, v, preferred_element_type=jnp.float32) + cross * xi)

    for t, o in enumerate(outs):
        rows = slice(t * C, (t + 1) * C)
        mu = jnp.mean(o, axis=-1, keepdims=True)
        oc = o - mu
        var = jnp.mean(oc * oc, axis=-1, keepdims=True)
        on = oc * lax.rsqrt(var + LN_EPS)
        o_ref[rows, :] = (on * _silu(g_ref[rows, :].astype(jnp.float32))).astype(o_ref.dtype)


def _retention(h_main):
    C = RET_CHUNK
    rows = C * RET_CHUNKS_PER_STEP
    ns = SEQ // rows
    return pl.pallas_call(
        _retention_kernel,
        grid=(RET_HEADS, BATCH, ns),
        in_specs=[
            pl.BlockSpec((rows, RET_DK), lambda h, b, c: (b * ns + c, MAIN_RQ // RET_DK + h)),
            pl.BlockSpec((rows, RET_DK), lambda h, b, c: (b * ns + c, MAIN_RK // RET_DK + h)),
            pl.BlockSpec((rows, RET_DV), lambda h, b, c: (b * ns + c, MAIN_RV // RET_DV + h)),
            pl.BlockSpec((rows, RET_DV), lambda h, b, c: (b * ns + c, MAIN_RG // RET_DV + h)),
        ],
        out_specs=pl.BlockSpec((rows, RET_DV), lambda h, b, c: (b * ns + c, h)),
        out_shape=jax.ShapeDtypeStruct((TOKENS, RET_W), jnp.bfloat16),
        scratch_shapes=[
            pltpu.VMEM((RET_DK, RET_DV), jnp.float32),
            pltpu.VMEM((C, C), jnp.float32),
            pltpu.VMEM((C, LANES), jnp.float32),
            pltpu.VMEM((C, LANES), jnp.float32),
            pltpu.VMEM((8, RET_DV), jnp.float32),
        ],
        compiler_params=_cparams(("arbitrary", "arbitrary", "arbitrary")),
        name="retention",
    )(h_main, h_main, h_main, h_main)


def _rms_norm(x, g):
    return x * lax.rsqrt(jnp.mean(x * x, axis=-1, keepdims=True) + RMS_EPS) * g


def _mla_prep_kernel(x_ref, ws_ref, qg_ref, kvg_ref, wuq_ref, wukv_ref, cos_ref, sin_ref,
                     q_ref, k_ref, v_ref):
    xb = x_ref[...].astype(jnp.bfloat16)
    hs = jnp.dot(xb, ws_ref[...], preferred_element_type=jnp.float32)
    mq = hs[:, :MLA_Q_RANK]
    mkv = hs[:, MLA_Q_RANK: MLA_Q_RANK + MLA_KV_RANK]
    mkr = hs[:, MLA_Q_RANK + MLA_KV_RANK:]
    cos = cos_ref[...]
    sin = sin_ref[...]

    qn = _rms_norm(mq, qg_ref[...]).astype(jnp.bfloat16)
    kvn = _rms_norm(mkv, kvg_ref[...]).astype(jnp.bfloat16)
    qm = jnp.dot(qn, wuq_ref[...], preferred_element_type=jnp.float32)
    kv = jnp.dot(kvn, wukv_ref[...], preferred_element_type=jnp.float32)

    kpe = (mkr * cos + pltpu.roll(mkr, HALF, axis=1) * sin).astype(k_ref.dtype)
    scale = (MLA_NOPE + MLA_ROPE) ** -0.5 * LOG2E
    lane = lax.broadcasted_iota(jnp.int32, (x_ref.shape[0], LANES), 1)
    second = (lane // (HALF // 2)) % 2
    nope_w = MLA_HEADS * MLA_NOPE
    for p in range(MLA_HEADS // 2):
        pe = qm[:, nope_w + p * LANES: nope_w + (p + 1) * LANES]
        pe = (pe * cos + pltpu.roll(pe, HALF, axis=1) * sin) * scale
        for e in range(2):
            h = 2 * p + e
            base = h * 2 * LANES
            q_ref[:, base: base + LANES] = (qm[:, h * LANES: (h + 1) * LANES] * scale).astype(q_ref.dtype)
            q_ref[:, base + LANES: base + 2 * LANES] = jnp.where(second == e, pe, 0.0).astype(q_ref.dtype)
            k_ref[:, base: base + LANES] = kv[:, h * LANES: (h + 1) * LANES].astype(k_ref.dtype)
            k_ref[:, base + LANES: base + 2 * LANES] = kpe
    v_ref[...] = kv[:, nope_w:].astype(v_ref.dtype)


def _mla_prep(x, w_small, q_g, kv_g, w_uq, w_ukv, cos_p, sin_p):
    tm = PREP_TM
    full = lambda a: _resident(a.shape)
    qk_w = MLA_HEADS * 2 * LANES
    return pl.pallas_call(
        _mla_prep_kernel,
        grid=(TOKENS // tm,),
        in_specs=[
            pl.BlockSpec((tm, D_MODEL), lambda i: (i, 0)),
            full(w_small), full(q_g), full(kv_g), full(w_uq), full(w_ukv),
            pl.BlockSpec((tm, LANES), lambda i: (i, 0)),
            pl.BlockSpec((tm, LANES), lambda i: (i, 0)),
        ],
        out_specs=[
            pl.BlockSpec((tm, qk_w), lambda i: (i, 0)),
            pl.BlockSpec((tm, qk_w), lambda i: (i, 0)),
            pl.BlockSpec((tm, MLA_W), lambda i: (i, 0)),
        ],
        out_shape=[
            jax.ShapeDtypeStruct((TOKENS, qk_w), jnp.bfloat16),
            jax.ShapeDtypeStruct((TOKENS, qk_w), jnp.bfloat16),
            jax.ShapeDtypeStruct((TOKENS, MLA_W), jnp.bfloat16),
        ],
        compiler_params=_cparams(("parallel",)),
        name="mla_prep",
    )(x, w_small, q_g, kv_g, w_uq, w_ukv, cos_p, sin_p)


def _mla_attn_kernel(q_ref, k_ref, v_ref, g_ref, o_ref):
    row = lax.broadcasted_iota(jnp.int32, (ATT_TQ, ATT_TQ), 0)
    col = lax.broadcasted_iota(jnp.int32, (ATT_TQ, ATT_TQ), 1)
    v_aug = jnp.concatenate([v_ref[...], jnp.ones((SEQ, LANES), jnp.bfloat16)], axis=1)

    def scores_softmax(i):
        r0 = i * ATT_TQ
        n = r0 + ATT_TQ
        s = lax.dot_general(q_ref[r0:n, :], k_ref[0:n, :], (((1,), (1,)), ((), ())),
                            preferred_element_type=jnp.float32)
        diag = jnp.where(col <= row, s[:, r0:], NEG)
        s = diag if i == 0 else jnp.concatenate([s[:, :r0], diag], axis=1)
        m = jnp.max(s, axis=-1, keepdims=True)
        return jnp.exp2(s - m).astype(jnp.bfloat16)

    def weighted_values(i, p):
        r0 = i * ATT_TQ
        n = r0 + ATT_TQ
        acc = jnp.dot(p, v_aug[0:n, :], preferred_element_type=jnp.float32)
        gate = _silu(g_ref[r0:n, :].astype(jnp.float32))
        o_ref[r0:n, :] = (acc[:, :MLA_V] / acc[:, MLA_V:] * gate).astype(o_ref.dtype)

    pending = []
    for i in range(SEQ // ATT_TQ):
        pending.append((i, scores_softmax(i)))
        if len(pending) > ATT_LOOKAHEAD:
            weighted_values(*pending.pop(0))
    for item in pending:
        weighted_values(*item)


def _mla_attention(q_full, k_full, v_m, h_main):
    qk_d = 2 * LANES
    return pl.pallas_call(
        _mla_attn_kernel,
        grid=(BATCH, MLA_HEADS),
        in_specs=[
            pl.BlockSpec((SEQ, qk_d), lambda b, h: (b, h)),
            pl.BlockSpec((SEQ, qk_d), lambda b, h: (b, h)),
            pl.BlockSpec((SEQ, MLA_V), lambda b, h: (b, h)),
            pl.BlockSpec((SEQ, MLA_V), lambda b, h: (b, MAIN_MG // MLA_V + h)),
        ],
        out_specs=pl.BlockSpec((SEQ, MLA_V), lambda b, h: (b, h)),
        out_shape=jax.ShapeDtypeStruct((TOKENS, MLA_W), jnp.bfloat16),
        compiler_params=_cparams(("parallel", "parallel")),
        name="mla_attention",
    )(q_full, k_full, v_m, h_main)


def _out_ln_kernel(*refs, feature_major):
    n_acts = len(feature_major)
    a_refs = refs[:n_acts]
    w_ref, x_ref, g_ref, b_ref, o_ref = refs[n_acts:]
    for r0 in range(0, o_ref.shape[0], OUT_SUB):
        y = None
        k0 = 0
        for a_ref, fm in zip(a_refs, feature_major):
            if fm:
                kw = a_ref.shape[0]
                t = lax.dot_general(a_ref[:, r0: r0 + OUT_SUB], w_ref[k0: k0 + kw, :], (((0,), (0,)), ((), ())),
                                    preferred_element_type=jnp.float32)
            else:
                kw = a_ref.shape[1]
                t = jnp.dot(a_ref[r0: r0 + OUT_SUB, :], w_ref[k0: k0 + kw, :], preferred_element_type=jnp.float32)
            y = t if y is None else y + t
            k0 += kw
        z = DN_ALPHA * x_ref[r0: r0 + OUT_SUB, :] + y
        mu = jnp.mean(z, axis=-1, keepdims=True)
        zc = z - mu
        var = jnp.mean(zc * zc, axis=-1, keepdims=True)
        o_ref[r0: r0 + OUT_SUB, :] = zc * lax.rsqrt(var + LN_EPS) * g_ref[...] + b_ref[...]


def _out_ln(acts, w, x, ln_g, ln_b, name, feature_major=None):
    tm = OUT_TM
    feature_major = tuple(feature_major or (False,) * len(acts))
    in_specs = [pl.BlockSpec((a.shape[0], tm), lambda i: (0, i)) if fm else
                pl.BlockSpec((tm, a.shape[1]), lambda i: (i, 0)) for a, fm in zip(acts, feature_major)]
    in_specs += [_resident(w.shape), pl.BlockSpec((tm, D_MODEL), lambda i: (i, 0)),
                 _resident(ln_g.shape), _resident(ln_b.shape)]
    return pl.pallas_call(
        functools.partial(_out_ln_kernel, feature_major=feature_major),
        grid=(TOKENS // tm,),
        in_specs=in_specs,
        out_specs=pl.BlockSpec((tm, D_MODEL), lambda i: (i, 0)),
        out_shape=jax.ShapeDtypeStruct((TOKENS, D_MODEL), jnp.float32),
        compiler_params=_cparams(("parallel",)),
        name=name,
    )(*acts, w, x, ln_g, ln_b)


def _split3(x):
    hi = x.astype(jnp.bfloat16)
    r1 = x - hi.astype(jnp.float32)
    mid = r1.astype(jnp.bfloat16)
    lo = (r1 - mid.astype(jnp.float32)).astype(jnp.bfloat16)
    return hi, mid, lo


def _moba_kernel(q_ref, k_ref, vt_ref, gt_ref, o_ref):
    L = MOBA_BLOCK
    NB = MOBA_NB
    means = [jnp.mean(k_ref[n * L: (n + 1) * L, :].astype(jnp.float32), axis=0, keepdims=True) for n in range(NB)]
    km = jnp.concatenate(means, axis=0)
    parts = [p.astype(jnp.float32) for p in _split3(km)]
    km3 = jnp.concatenate(parts + [jnp.zeros_like(km)], axis=0).astype(jnp.bfloat16)

    q = q_ref[...]
    lane = lax.broadcasted_iota(jnp.int32, (SEQ, LANES), 1)
    second = (lane // (HALF // 2)) % 2
    blk = lax.broadcasted_iota(jnp.int32, (NB, SEQ), 0)
    qblk = lax.broadcasted_iota(jnp.int32, (NB, SEQ), 1) // L
    past = blk < qblk
    krow = lax.broadcasted_iota(jnp.int32, (L, L), 0)
    qcol = lax.broadcasted_iota(jnp.int32, (L, L), 1)
    hd = MOBA_DH
    qes, sels = [], []
    for e in range(2):
        qe = jnp.where(second == e, q, jnp.zeros_like(q))
        g3 = lax.dot_general(km3, qe, (((1,), (1,)), ((), ())), preferred_element_type=jnp.float32)
        gate = jnp.where(past, g3[0:NB] + g3[NB: 2 * NB] + g3[2 * NB: 3 * NB], NEG)
        rank = jnp.zeros((NB, SEQ), jnp.float32)
        for n in range(NB - 1):
            gn = gate[n: n + 1, :]
            rank = rank + jnp.where(gn > gate, 1.0, jnp.where(gn == gate, jnp.where(blk > n, 1.0, 0.0), 0.0))
        qes.append(qe)
        sels.append(jnp.where(past, jnp.where(rank < float(MOBA_TOPK), 1.0, 0.0), 0.0))

    def scores_softmax(e, i):
        r0 = i * L
        n_keys = r0 + L
        st = lax.dot_general(k_ref[0:n_keys, :], qes[e][r0:n_keys, :], (((1,), (1,)), ((), ())),
                             preferred_element_type=jnp.float32)
        pieces = [jnp.where(sels[e][j: j + 1, r0:n_keys] > 0.0, st[j * L: (j + 1) * L, :], NEG) for j in range(i)]
        pieces.append(jnp.where(krow <= qcol, st[r0:, :], NEG))
        sm = pieces[0] if i == 0 else jnp.concatenate(pieces, axis=0)
        m = jnp.max(sm, axis=0, keepdims=True)
        return (jnp.exp2(sm - m).astype(jnp.bfloat16),)

    ones = jnp.ones((BF16_SUBLANES, SEQ), jnp.bfloat16)
    vt_aug = [jnp.concatenate([vt_ref[e * hd: (e + 1) * hd, :], ones], axis=0) for e in range(2)]

    def weighted_values(e, i, p):
        r0 = i * L
        n_keys = r0 + L
        ot = jnp.dot(vt_aug[e][:, 0:n_keys], p, preferred_element_type=jnp.float32)
        gate_t = _silu(gt_ref[e * hd: (e + 1) * hd, r0:n_keys].astype(jnp.float32))
        o_ref[e * hd: (e + 1) * hd, r0:n_keys] = (ot[:hd] / ot[hd: hd + 1] * gate_t).astype(o_ref.dtype)

    units = [(e, i) for i in range(NB) for e in range(2)]
    pending = []
    for unit in units:
        pending.append(unit + scores_softmax(*unit))
        if len(pending) > MOBA_LOOKAHEAD:
            weighted_values(*pending.pop(0))
    for item in pending:
        weighted_values(*item)


def _moba(h_qk, h_vg_t):
    npair = MOBA_HEADS // 2
    return pl.pallas_call(
        _moba_kernel,
        grid=(BATCH, npair),
        in_specs=[
            pl.BlockSpec((SEQ, LANES), lambda b, p: (b, p)),
            pl.BlockSpec((SEQ, LANES), lambda b, p: (b, npair + p)),
            pl.BlockSpec((LANES, SEQ), lambda b, p: (p, b)),
            pl.BlockSpec((LANES, SEQ), lambda b, p: (npair + p, b)),
        ],
        out_specs=pl.BlockSpec((LANES, SEQ), lambda b, p: (p, b)),
        out_shape=jax.ShapeDtypeStruct((MOBA_W, TOKENS), jnp.bfloat16),
        compiler_params=_cparams(("parallel", "parallel")),
        name="moba_attention",
    )(h_qk, h_qk, h_vg_t, h_vg_t)


def _retnet_perm():
    per_head = np.concatenate([np.arange(0, RET_DK, 2), np.arange(1, RET_DK, 2)])
    return np.concatenate([h * RET_DK + per_head for h in range(RET_HEADS)])


def _pair_perm(n_heads, head_w, rope_off):
    q = MOBA_DH // 2
    cols = []
    for p in range(n_heads // 2):
        a = (2 * p) * head_w + rope_off
        b = (2 * p + 1) * head_w + rope_off
        cols += [np.arange(a, a + q), np.arange(b, b + q), np.arange(a + q, a + 2 * q), np.arange(b + q, b + 2 * q)]
    return np.concatenate(cols)


def kernel(x, positions, w_in_even, q_norm_even, w_uq_even, kv_norm_even, w_ukv_even, w_out_even, w_in_odd,
           w_out_odd, ln_g, ln_b):
    bf = jnp.bfloat16
    x0 = x.reshape(TOKENS, D_MODEL)
    pos = positions.reshape(TOKENS, 1)

    inv_r = (1.0 / (ROPE_THETA ** jnp.linspace(0.0, 1.0, RET_DK // 2, dtype=jnp.float32))).reshape(1, LANES)
    inv_rope = 1.0 / (ROPE_THETA ** (jnp.arange(0, MOBA_DH, 2, dtype=jnp.float32) / MOBA_DH))
    inv_p = jnp.tile(inv_rope, LANES // (MOBA_DH // 2)).reshape(1, LANES)
    cos_r, sin_r, cos_p, sin_p = _tables(pos, inv_r, inv_p)

    take = lambda w, cols: jnp.take(w, jnp.asarray(cols, jnp.int32), axis=1).astype(bf)
    o = np.cumsum((0, 1024, 1024, RET_W, RET_W, MLA_Q_RANK, MLA_KV_RANK, MLA_ROPE, MLA_W))
    rperm = _retnet_perm()
    main_cols = np.concatenate([o[0] + rperm, o[1] + rperm, np.arange(o[2], o[4]), np.arange(o[7], o[8])])
    q32 = MLA_ROPE // 2
    kr_dup = np.concatenate([np.arange(q32), np.arange(q32), np.arange(q32, 2 * q32), np.arange(q32, 2 * q32)])
    small_cols = np.concatenate([np.arange(o[4], o[6]), o[6] + kr_dup])
    w_main = take(w_in_even[0], main_cols)
    w_small = take(w_in_even[0], small_cols)
    hw = MLA_NOPE + MLA_ROPE
    nope_cols = np.concatenate([np.arange(h * hw, h * hw + MLA_NOPE) for h in range(MLA_HEADS)])
    w_uq_p = take(w_uq_even[0], np.concatenate([nope_cols, _pair_perm(MLA_HEADS, hw, MLA_NOPE)]))
    kvw = MLA_NOPE + MLA_V
    knope_cols = np.concatenate([np.arange(h * kvw, h * kvw + MLA_NOPE) for h in range(MLA_HEADS)])
    v_cols = np.concatenate([np.arange(h * kvw + MLA_NOPE, (h + 1) * kvw) for h in range(MLA_HEADS)])
    w_ukv_p = take(w_ukv_even[0], np.concatenate([knope_cols, v_cols]))
    pperm = _pair_perm(MOBA_HEADS, MOBA_DH, 0)
    w_qk = take(w_in_odd[0], np.concatenate([pperm, MOBA_W + pperm]))
    w_vg_t = w_in_odd[0][:, 2 * MOBA_W:].T.astype(bf)

    h_main = _project_even(x0, w_main, cos_r, sin_r)
    ret_out = _retention(h_main)
    q_full, k_full, v_m = _mla_prep(x0, w_small, q_norm_even[0].reshape(1, -1), kv_norm_even[0].reshape(1, -1),
                                    w_uq_p, w_ukv_p, cos_p, sin_p)
    mla_out = _mla_attention(q_full, k_full, v_m, h_main)
    x1 = _out_ln([ret_out, mla_out], w_out_even[0].astype(bf), x0,
                 ln_g[0].reshape(1, -1), ln_b[0].reshape(1, -1), name="out_ln_even")

    h_qk, h_vg_t = _project_odd(x1, w_qk, w_vg_t, cos_p, sin_p)
    moba_out_t = _moba(h_qk, h_vg_t)
    x2 = _out_ln([moba_out_t], w_out_odd[0].astype(bf), x1, ln_g[1].reshape(1, -1), ln_b[1].reshape(1, -1),
                 name="out_ln_odd", feature_major=(True,))
    return x2.reshape(BATCH, SEQ, D_MODEL)
```

```python
import functools
import math

import numpy as np
import jax
import jax.numpy as jnp
from jax import lax
from jax.experimental import pallas as pl
from jax.experimental.pallas import tpu as pltpu

D_MODEL = 1024
BATCH = 8
SEQ = 2048
DEPTH = 2
TOKENS = BATCH * SEQ

RET_HEADS = 4
RET_DK = 256
RET_DV = 512
MLA_HEADS = 8
MLA_NOPE = 128
MLA_ROPE = 64
MLA_V = 128
MLA_Q_RANK = 256
MLA_KV_RANK = 256
MOBA_HEADS = 16
MOBA_DH = 64
MOBA_BLOCK = 256
MOBA_TOPK = 3
MOBA_NB = SEQ // MOBA_BLOCK
MOBA_LOOKAHEAD = 2
ATT_LOOKAHEAD = 2

ROPE_THETA = 10000.0
LN_EPS = 1e-5
RMS_EPS = 1e-6
NEG = -1e30
DN_ALPHA = (2.0 * DEPTH) ** 0.25
LOG2E = math.log2(math.e)

RET_W = RET_HEADS * RET_DV
MLA_W = MLA_HEADS * MLA_V
MOBA_W = MOBA_HEADS * MOBA_DH

LANES = 128
HALF = LANES // 2
BF16_SUBLANES = 16
VMEM_LIMIT = 48 * 1024 * 1024

MAIN_RQ, MAIN_RK, MAIN_RV, MAIN_RG, MAIN_MG = 0, 1024, 2048, 4096, 6144
MAIN_W = 7168
SMALL_W = MLA_Q_RANK + MLA_KV_RANK + LANES

PROJ_TM = 512
PROJ_TN = 1024
OUT_SUB = 256
RET_CHUNK = 256
RET_CHUNKS_PER_STEP = 2
ATT_TQ = 256
ATT_TK = 256
PREP_TM = 512
OUT_TM = 512
TAB_TM = 1024


def _cparams(sem):
    return pltpu.CompilerParams(dimension_semantics=sem, vmem_limit_bytes=VMEM_LIMIT)


def _silu(g):
    h = 0.5 * g
    return h + h * jnp.tanh(h)


def _tables_kernel(pos_ref, inv_r_ref, inv_p_ref, cr_ref, sr_ref, cp_ref, sp_ref):
    pos = pos_ref[...].astype(jnp.float32)
    ang_r = pos * inv_r_ref[...]
    cr_ref[...] = jnp.cos(ang_r)
    sr_ref[...] = jnp.sin(ang_r)
    ang_p = pos * inv_p_ref[...]
    lane = lax.broadcasted_iota(jnp.int32, ang_p.shape, 1)
    cp_ref[...] = jnp.cos(ang_p)
    sin_p = jnp.sin(ang_p)
    sp_ref[...] = jnp.where(lane < HALF, -sin_p, sin_p)


def _tables(pos, inv_r, inv_p):
    tab = jax.ShapeDtypeStruct((TOKENS, LANES), jnp.float32)
    row = pl.BlockSpec((TAB_TM, LANES), lambda i: (i, 0))
    cst = pl.BlockSpec((1, LANES), lambda i: (0, 0))
    return pl.pallas_call(
        _tables_kernel,
        grid=(TOKENS // TAB_TM,),
        in_specs=[pl.BlockSpec((TAB_TM, 1), lambda i: (i, 0)), cst, cst],
        out_specs=[row, row, row, row],
        out_shape=[tab, tab, tab, tab],
        compiler_params=_cparams(("parallel",)),
        name="rot_tables",
    )(pos, inv_r, inv_p)


def _resident(shape):
    return pl.BlockSpec(shape, lambda i: (0,) * len(shape), pipeline_mode=pl.Buffered(1))


def _proj_even_kernel(x_ref, w_ref, cos_ref, sin_ref, o_ref):
    xb = x_ref[...].astype(jnp.bfloat16)
    cos = cos_ref[...]
    sin = sin_ref[...]
    for c0 in range(0, MAIN_W, PROJ_TN):
        acc = jnp.dot(xb, w_ref[:, c0: c0 + PROJ_TN], preferred_element_type=jnp.float32)
        if c0 not in (MAIN_RQ, MAIN_RK):
            o_ref[:, c0: c0 + PROJ_TN] = acc.astype(o_ref.dtype)
            continue
        for h0 in range(0, PROJ_TN, RET_DK):
            x1 = acc[:, h0: h0 + LANES]
            x2 = acc[:, h0 + LANES: h0 + RET_DK]
            r1 = x1 * cos - x2 * sin
            r2 = x2 * cos + x1 * sin
            if c0 == MAIN_RK:
                r1 = r1 * RET_DK ** -0.5
                r2 = r2 * RET_DK ** -0.5
            o_ref[:, c0 + h0: c0 + h0 + LANES] = r1.astype(o_ref.dtype)
            o_ref[:, c0 + h0 + LANES: c0 + h0 + RET_DK] = r2.astype(o_ref.dtype)


def _project_even(x, w, cos, sin):
    tm = PROJ_TM
    row = lambda width: pl.BlockSpec((tm, width), lambda i: (i, 0))
    return pl.pallas_call(
        _proj_even_kernel,
        grid=(TOKENS // tm,),
        in_specs=[row(D_MODEL), _resident(w.shape), row(LANES), row(LANES)],
        out_specs=row(MAIN_W),
        out_shape=jax.ShapeDtypeStruct((TOKENS, MAIN_W), jnp.bfloat16),
        compiler_params=_cparams(("parallel",)),
        name="proj_even",
    )(x, w, cos, sin)


def _proj_odd_kernel(x_ref, wqk_ref, wvgt_ref, cos_ref, sin_ref, oqk_ref, ovgt_ref):
    xb = x_ref[...].astype(jnp.bfloat16)
    cos = cos_ref[...]
    sin = sin_ref[...]
    for c0 in range(0, 2 * MOBA_W, PROJ_TN):
        acc = jnp.dot(xb, wqk_ref[:, c0: c0 + PROJ_TN], preferred_element_type=jnp.float32)
        for g0 in range(0, PROJ_TN, LANES):
            xg = acc[:, g0: g0 + LANES]
            rot = xg * cos + pltpu.roll(xg, HALF, axis=1) * sin
            if c0 < MOBA_W:
                rot = rot * (MOBA_DH ** -0.5 * LOG2E)
            oqk_ref[:, c0 + g0: c0 + g0 + LANES] = rot.astype(oqk_ref.dtype)
    for r0 in range(0, 2 * MOBA_W, PROJ_TN):
        ovgt_ref[r0: r0 + PROJ_TN, :] = lax.dot_general(
            wvgt_ref[r0: r0 + PROJ_TN, :], xb, (((1,), (1,)), ((), ())),
            preferred_element_type=jnp.float32).astype(ovgt_ref.dtype)


def _project_odd(x, w_qk, w_vg_t, cos, sin):
    tm = PROJ_TM
    row = lambda width: pl.BlockSpec((tm, width), lambda i: (i, 0))
    return pl.pallas_call(
        _proj_odd_kernel,
        grid=(TOKENS // tm,),
        in_specs=[row(D_MODEL), _resident(w_qk.shape), _resident(w_vg_t.shape), row(LANES), row(LANES)],
        out_specs=[row(2 * MOBA_W), pl.BlockSpec((2 * MOBA_W, tm), lambda i: (0, i))],
        out_shape=[jax.ShapeDtypeStruct((TOKENS, 2 * MOBA_W), jnp.bfloat16),
                   jax.ShapeDtypeStruct((2 * MOBA_W, TOKENS), jnp.bfloat16)],
        compiler_params=_cparams(("parallel",)),
        name="proj_odd",
    )(x, w_qk, w_vg_t, cos, sin)


def _retention_kernel(q_ref, k_ref, v_ref, g_ref, o_ref, state_ref, decay_ref, xi_ref, zeta_ref, gch_ref):
    C = RET_CHUNK

    @pl.when((pl.program_id(1) == 0) & (pl.program_id(2) == 0))
    def _():
        def log_gamma(shape):
            hf = jnp.full(shape, pl.program_id(0), jnp.int32).astype(jnp.float32)
            return jnp.log(1.0 - jnp.exp2(-5.0 - hf))

        row = lax.broadcasted_iota(jnp.int32, (C, C), 0)
        col = lax.broadcasted_iota(jnp.int32, (C, C), 1)
        rel = (row - col).astype(jnp.float32)
        decay_ref[...] = jnp.where(rel >= 0, jnp.exp(log_gamma((C, C)) * jnp.maximum(rel, 0.0)), 0.0)
        idx = lax.broadcasted_iota(jnp.int32, (C, LANES), 0).astype(jnp.float32)
        xi_ref[...] = jnp.exp(log_gamma((C, LANES)) * (idx + 1.0))
        zeta_ref[...] = jnp.exp(log_gamma((C, LANES)) * (C - 1.0 - idx))
        gch_ref[...] = jnp.exp(log_gamma(gch_ref.shape) * float(C))

    @pl.when(pl.program_id(2) == 0)
    def _():
        state_ref[...] = jnp.zeros_like(state_ref)

    decay = decay_ref[...]
    xi = jnp.concatenate([xi_ref[...]] * (RET_DV // LANES), axis=1)
    zeta = jnp.concatenate([zeta_ref[...]] * (RET_DK // LANES), axis=1)
    g_chunk = gch_ref[0:1, :]

    outs = []
    for t in range(RET_CHUNKS_PER_STEP):
        rows = slice(t * C, (t + 1) * C)
        q = q_ref[rows, :]
        k = k_ref[rows, :]
        v = v_ref[rows, :]
        s = lax.dot_general(q, k, (((1,), (1,)), ((), ())), preferred_element_type=jnp.float32)
        state = state_ref[...]
        cross = jnp.dot(q, state.astype(jnp.bfloat16), preferred_element_type=jnp.float32)
        kz = (k.astype(jnp.float32) * zeta).astype(jnp.bfloat16)
        kv = lax.dot_general(kz, v, (((0,), (0,)), ((), ())), preferred_element_type=jnp.float32)
        state_ref[...] = state * g_chunk + kv
        inner = (s * decay).astype(jnp.bfloat16)
        outs.append(jnp.dot(inner, v, preferred_element_type=jnp.float32) + cross * xi)

    for t, o in enumerate(outs):
        rows = slice(t * C, (t + 1) * C)
        mu = jnp.mean(o, axis=-1, keepdims=True)
        oc = o - mu
        var = jnp.mean(oc * oc, axis=-1, keepdims=True)
        on = oc * lax.rsqrt(var + LN_EPS)
        o_ref[rows, :] = (on * _silu(g_ref[rows, :].astype(jnp.float32))).astype(o_ref.dtype)


def _retention(h_main):
    C = RET_CHUNK
    rows = C * RET_CHUNKS_PER_STEP
    ns = SEQ // rows
    return pl.pallas_call(
        _retention_kernel,
        grid=(RET_HEADS, BATCH, ns),
        in_specs=[
            pl.BlockSpec((rows, RET_DK), lambda h, b, c: (b * ns + c, MAIN_RQ // RET_DK + h)),
            pl.BlockSpec((rows, RET_DK), lambda h, b, c: (b * ns + c, MAIN_RK // RET_DK + h)),
            pl.BlockSpec((rows, RET_DV), lambda h, b, c: (b * ns + c, MAIN_RV // RET_DV + h)),
            pl.BlockSpec((rows, RET_DV), lambda h, b, c: (b * ns + c, MAIN_RG // RET_DV + h)),
        ],
        out_specs=pl.BlockSpec((rows, RET_DV), lambda h, b, c: (b * ns + c, h)),
        out_shape=jax.ShapeDtypeStruct((TOKENS, RET_W), jnp.bfloat16),
        scratch_shapes=[
            pltpu.VMEM((RET_DK, RET_DV), jnp.float32),
            pltpu.VMEM((C, C), jnp.float32),
            pltpu.VMEM((C, LANES), jnp.float32),
            pltpu.VMEM((C, LANES), jnp.float32),
            pltpu.VMEM((8, RET_DV), jnp.float32),
        ],
        compiler_params=_cparams(("arbitrary", "arbitrary", "arbitrary")),
        name="retention",
    )(h_main, h_main, h_main, h_main)


def _rms_norm(x, g):
    return x * lax.rsqrt(jnp.mean(x * x, axis=-1, keepdims=True) + RMS_EPS) * g


def _mla_prep_kernel(x_ref, ws_ref, qg_ref, kvg_ref, wuq_ref, wukv_ref, cos_ref, sin_ref,
                     q_ref, k_ref, v_ref):
    xb = x_ref[...].astype(jnp.bfloat16)
    hs = jnp.dot(xb, ws_ref[...], preferred_element_type=jnp.float32)
    mq = hs[:, :MLA_Q_RANK]
    mkv = hs[:, MLA_Q_RANK: MLA_Q_RANK + MLA_KV_RANK]
    mkr = hs[:, MLA_Q_RANK + MLA_KV_RANK:]
    cos = cos_ref[...]
    sin = sin_ref[...]

    qn = _rms_norm(mq, qg_ref[...]).astype(jnp.bfloat16)
    kvn = _rms_norm(mkv, kvg_ref[...]).astype(jnp.bfloat16)
    qm = jnp.dot(qn, wuq_ref[...], preferred_element_type=jnp.float32)
    kv = jnp.dot(kvn, wukv_ref[...], preferred_element_type=jnp.float32)

    kpe = (mkr * cos + pltpu.roll(mkr, HALF, axis=1) * sin).astype(k_ref.dtype)
    scale = (MLA_NOPE + MLA_ROPE) ** -0.5 * LOG2E
    lane = lax.broadcasted_iota(jnp.int32, (x_ref.shape[0], LANES), 1)
    second = (lane // (HALF // 2)) % 2
    nope_w = MLA_HEADS * MLA_NOPE
    for p in range(MLA_HEADS // 2):
        pe = qm[:, nope_w + p * LANES: nope_w + (p + 1) * LANES]
        pe = (pe * cos + pltpu.roll(pe, HALF, axis=1) * sin) * scale
        for e in range(2):
            h = 2 * p + e
            base = h * 2 * LANES
            q_ref[:, base: base + LANES] = (qm[:, h * LANES: (h + 1) * LANES] * scale).astype(q_ref.dtype)
            q_ref[:, base + LANES: base + 2 * LANES] = jnp.where(second == e, pe, 0.0).astype(q_ref.dtype)
            k_ref[:, base: base + LANES] = kv[:, h * LANES: (h + 1) * LANES].astype(k_ref.dtype)
            k_ref[:, base + LANES: base + 2 * LANES] = kpe
    v_ref[...] = kv[:, nope_w:].astype(v_ref.dtype)


def _mla_prep(x, w_small, q_g, kv_g, w_uq, w_ukv, cos_p, sin_p):
    tm = PREP_TM
    full = lambda a: _resident(a.shape)
    qk_w = MLA_HEADS * 2 * LANES
    return pl.pallas_call(
        _mla_prep_kernel,
        grid=(TOKENS // tm,),
        in_specs=[
            pl.BlockSpec((tm, D_MODEL), lambda i: (i, 0)),
            full(w_small), full(q_g), full(kv_g), full(w_uq), full(w_ukv),
            pl.BlockSpec((tm, LANES), lambda i: (i, 0)),
            pl.BlockSpec((tm, LANES), lambda i: (i, 0)),
        ],
        out_specs=[
            pl.BlockSpec((tm, qk_w), lambda i: (i, 0)),
            pl.BlockSpec((tm, qk_w), lambda i: (i, 0)),
            pl.BlockSpec((tm, MLA_W), lambda i: (i, 0)),
        ],
        out_shape=[
            jax.ShapeDtypeStruct((TOKENS, qk_w), jnp.bfloat16),
            jax.ShapeDtypeStruct((TOKENS, qk_w), jnp.bfloat16),
            jax.ShapeDtypeStruct((TOKENS, MLA_W), jnp.bfloat16),
        ],
        compiler_params=_cparams(("parallel",)),
        name="mla_prep",
    )(x, w_small, q_g, kv_g, w_uq, w_ukv, cos_p, sin_p)


def _mla_attn_kernel(q_ref, k_ref, v_ref, g_ref, o_ref):
    row = lax.broadcasted_iota(jnp.int32, (ATT_TQ, ATT_TQ), 0)
    col = lax.broadcasted_iota(jnp.int32, (ATT_TQ, ATT_TQ), 1)
    v_aug = jnp.concatenate([v_ref[...], jnp.ones((SEQ, LANES), jnp.bfloat16)], axis=1)

    def scores_softmax(i):
        r0 = i * ATT_TQ
        n = r0 + ATT_TQ
        s = lax.dot_general(q_ref[r0:n, :], k_ref[0:n, :], (((1,), (1,)), ((), ())),
                            preferred_element_type=jnp.float32)
        diag = jnp.where(col <= row, s[:, r0:], NEG)
        s = diag if i == 0 else jnp.concatenate([s[:, :r0], diag], axis=1)
        m = jnp.max(s, axis=-1, keepdims=True)
        return jnp.exp2(s - m).astype(jnp.bfloat16)

    def weighted_values(i, p):
        r0 = i * ATT_TQ
        n = r0 + ATT_TQ
        acc = jnp.dot(p, v_aug[0:n, :], preferred_element_type=jnp.float32)
        gate = _silu(g_ref[r0:n, :].astype(jnp.float32))
        o_ref[r0:n, :] = (acc[:, :MLA_V] / acc[:, MLA_V:] * gate).astype(o_ref.dtype)

    pending = []
    for i in range(SEQ // ATT_TQ):
        pending.append((i, scores_softmax(i)))
        if len(pending) > ATT_LOOKAHEAD:
            weighted_values(*pending.pop(0))
    for item in pending:
        weighted_values(*item)


def _mla_attention(q_full, k_full, v_m, h_main):
    qk_d = 2 * LANES
    return pl.pallas_call(
        _mla_attn_kernel,
        grid=(BATCH, MLA_HEADS),
        in_specs=[
            pl.BlockSpec((SEQ, qk_d), lambda b, h: (b, h)),
            pl.BlockSpec((SEQ, qk_d), lambda b, h: (b, h)),
            pl.BlockSpec((SEQ, MLA_V), lambda b, h: (b, h)),
            pl.BlockSpec((SEQ, MLA_V), lambda b, h: (b, MAIN_MG // MLA_V + h)),
        ],
        out_specs=pl.BlockSpec((SEQ, MLA_V), lambda b, h: (b, h)),
        out_shape=jax.ShapeDtypeStruct((TOKENS, MLA_W), jnp.bfloat16),
        compiler_params=_cparams(("parallel", "parallel")),
        name="mla_attention",
    )(q_full, k_full, v_m, h_main)


def _out_ln_kernel(*refs, feature_major):
    n_acts = len(feature_major)
    a_refs = refs[:n_acts]
    w_ref, x_ref, g_ref, b_ref, o_ref = refs[n_acts:]
    for r0 in range(0, o_ref.shape[0], OUT_SUB):
        y = None
        k0 = 0
        for a_ref, fm in zip(a_refs, feature_major):
            if fm:
                kw = a_ref.shape[0]
                t = lax.dot_general(a_ref[:, r0: r0 + OUT_SUB], w_ref[k0: k0 + kw, :], (((0,), (0,)), ((), ())),
                                    preferred_element_type=jnp.float32)
            else:
                kw = a_ref.shape[1]
                t = jnp.dot(a_ref[r0: r0 + OUT_SUB, :], w_ref[k0: k0 + kw, :], preferred_element_type=jnp.float32)
            y = t if y is None else y + t
            k0 += kw
        z = DN_ALPHA * x_ref[r0: r0 + OUT_SUB, :] + y
        mu = jnp.mean(z, axis=-1, keepdims=True)
        zc = z - mu
        var = jnp.mean(zc * zc, axis=-1, keepdims=True)
        o_ref[r0: r0 + OUT_SUB, :] = zc * lax.rsqrt(var + LN_EPS) * g_ref[...] + b_ref[...]


def _out_ln(acts, w, x, ln_g, ln_b, name, feature_major=None):
    tm = OUT_TM
    feature_major = tuple(feature_major or (False,) * len(acts))
    in_specs = [pl.BlockSpec((a.shape[0], tm), lambda i: (0, i)) if fm else
                pl.BlockSpec((tm, a.shape[1]), lambda i: (i, 0)) for a, fm in zip(acts, feature_major)]
    in_specs += [_resident(w.shape), pl.BlockSpec((tm, D_MODEL), lambda i: (i, 0)),
                 _resident(ln_g.shape), _resident(ln_b.shape)]
    return pl.pallas_call(
        functools.partial(_out_ln_kernel, feature_major=feature_major),
        grid=(TOKENS // tm,),
        in_specs=in_specs,
        out_specs=pl.BlockSpec((tm, D_MODEL), lambda i: (i, 0)),
        out_shape=jax.ShapeDtypeStruct((TOKENS, D_MODEL), jnp.float32),
        compiler_params=_cparams(("parallel",)),
        name=name,
    )(*acts, w, x, ln_g, ln_b)


def _split3(x):
    hi = x.astype(jnp.bfloat16)
    r1 = x - hi.astype(jnp.float32)
    mid = r1.astype(jnp.bfloat16)
    lo = (r1 - mid.astype(jnp.float32)).astype(jnp.bfloat16)
    return hi, mid, lo


def _moba_kernel(q_ref, k_ref, vt_ref, gt_ref, o_ref):
    L = MOBA_BLOCK
    NB = MOBA_NB
    means = [jnp.mean(k_ref[n * L: (n + 1) * L, :].astype(jnp.float32), axis=0, keepdims=True) for n in range(NB)]
    km = jnp.concatenate(means, axis=0)
    parts = [p.astype(jnp.float32) for p in _split3(km)]
    km3 = jnp.concatenate(parts + [jnp.zeros_like(km)], axis=0).astype(jnp.bfloat16)

    q = q_ref[...]
    lane = lax.broadcasted_iota(jnp.int32, (SEQ, LANES), 1)
    second = (lane // (HALF // 2)) % 2
    blk = lax.broadcasted_iota(jnp.int32, (NB, SEQ), 0)
    qblk = lax.broadcasted_iota(jnp.int32, (NB, SEQ), 1) // L
    past = blk < qblk
    krow = lax.broadcasted_iota(jnp.int32, (L, L), 0)
    qcol = lax.broadcasted_iota(jnp.int32, (L, L), 1)
    hd = MOBA_DH
    qes, sels = [], []
    for e in range(2):
        qe = jnp.where(second == e, q, jnp.zeros_like(q))
        g3 = lax.dot_general(km3, qe, (((1,), (1,)), ((), ())), preferred_element_type=jnp.float32)
        gate = jnp.where(past, g3[0:NB] + g3[NB: 2 * NB] + g3[2 * NB: 3 * NB], NEG)
        rank = jnp.zeros((NB, SEQ), jnp.float32)
        for n in range(NB - 1):
            gn = gate[n: n + 1, :]
            rank = rank + jnp.where(gn > gate, 1.0, jnp.where(gn == gate, jnp.where(blk > n, 1.0, 0.0), 0.0))
        qes.append(qe)
        sels.append(jnp.where(past, jnp.where(rank < float(MOBA_TOPK), 1.0, 0.0), 0.0))

    def scores_softmax(e, i):
        r0 = i * L
        n_keys = r0 + L
        st = lax.dot_general(k_ref[0:n_keys, :], qes[e][r0:n_keys, :], (((1,), (1,)), ((), ())),
                             preferred_element_type=jnp.float32)
        pieces = [jnp.where(sels[e][j: j + 1, r0:n_keys] > 0.0, st[j * L: (j + 1) * L, :], NEG) for j in range(i)]
        pieces.append(jnp.where(krow <= qcol, st[r0:, :], NEG))
        sm = pieces[0] if i == 0 else jnp.concatenate(pieces, axis=0)
        m = jnp.max(sm, axis=0, keepdims=True)
        return (jnp.exp2(sm - m).astype(jnp.bfloat16),)

    ones = jnp.ones((BF16_SUBLANES, SEQ), jnp.bfloat16)
    vt_aug = [jnp.concatenate([vt_ref[e * hd: (e + 1) * hd, :], ones], axis=0) for e in range(2)]

    def weighted_values(e, i, p):
        r0 = i * L
        n_keys = r0 + L
        ot = jnp.dot(vt_aug[e][:, 0:n_keys], p, preferred_element_type=jnp.float32)
        gate_t = _silu(gt_ref[e * hd: (e + 1) * hd, r0:n_keys].astype(jnp.float32))
        o_ref[e * hd: (e + 1) * hd, r0:n_keys] = (ot[:hd] / ot[hd: hd + 1] * gate_t).astype(o_ref.dtype)

    units = [(e, i) for i in reversed(range(NB)) for e in range(2)]
    pending = []
    for unit in units:
        pending.append(unit + scores_softmax(*unit))
        if len(pending) > MOBA_LOOKAHEAD:
            weighted_values(*pending.pop(0))
    for item in pending:
        weighted_values(*item)


def _moba(h_qk, h_vg_t):
    npair = MOBA_HEADS // 2
    return pl.pallas_call(
        _moba_kernel,
        grid=(BATCH, npair),
        in_specs=[
            pl.BlockSpec((SEQ, LANES), lambda b, p: (b, p)),
            pl.BlockSpec((SEQ, LANES), lambda b, p: (b, npair + p)),
            pl.BlockSpec((LANES, SEQ), lambda b, p: (p, b)),
            pl.BlockSpec((LANES, SEQ), lambda b, p: (npair + p, b)),
        ],
        out_specs=pl.BlockSpec((LANES, SEQ), lambda b, p: (p, b)),
        out_shape=jax.ShapeDtypeStruct((MOBA_W, TOKENS), jnp.bfloat16),
        compiler_params=_cparams(("parallel", "parallel")),
        name="moba_attention",
    )(h_qk, h_qk, h_vg_t, h_vg_t)


def kernel(x, positions, w_in_even, q_norm_even, w_uq_even, kv_norm_even, w_ukv_even, w_out_even, w_in_odd,
           w_out_odd, ln_g, ln_b):
    bf = jnp.bfloat16
    x0 = x.reshape(TOKENS, D_MODEL)
    pos = positions.reshape(TOKENS, 1)

    inv_r = (1.0 / (ROPE_THETA ** jnp.linspace(0.0, 1.0, RET_DK // 2, dtype=jnp.float32))).reshape(1, LANES)
    inv_rope = 1.0 / (ROPE_THETA ** (jnp.arange(0, MOBA_DH, 2, dtype=jnp.float32) / MOBA_DH))
    inv_p = jnp.tile(inv_rope, LANES // (MOBA_DH // 2)).reshape(1, LANES)
    cos_r, sin_r, cos_p, sin_p = _tables(pos, inv_r, inv_p)

    def pair_layout(w):
        k, n = w.shape
        return w.reshape(k, n // LANES, 2, 2, HALF // 2).transpose(0, 1, 3, 2, 4).reshape(k, n)

    w_in = w_in_even[0]
    o = np.cumsum((0, 1024, 1024, RET_W, RET_W, MLA_Q_RANK, MLA_KV_RANK, MLA_ROPE, MLA_W))
    rot = w_in[:, :o[2]].reshape(D_MODEL, 2 * RET_HEADS, RET_DK // 2, 2).swapaxes(2, 3).reshape(D_MODEL, o[2])
    w_main = jnp.concatenate([rot, w_in[:, o[2]: o[4]], w_in[:, o[7]: o[8]]], axis=1).astype(bf)
    mkr = w_in[:, o[6]: o[7]].reshape(D_MODEL, 2, 1, MLA_ROPE // 2)
    kr_dup = jnp.broadcast_to(mkr, (D_MODEL, 2, 2, MLA_ROPE // 2)).reshape(D_MODEL, LANES)
    w_small = jnp.concatenate([w_in[:, o[4]: o[6]], kr_dup], axis=1).astype(bf)
    w_uq = w_uq_even[0].reshape(MLA_Q_RANK, MLA_HEADS, MLA_NOPE + MLA_ROPE)
    w_uq_p = jnp.concatenate([w_uq[:, :, :MLA_NOPE].reshape(MLA_Q_RANK, -1),
                              pair_layout(w_uq[:, :, MLA_NOPE:].reshape(MLA_Q_RANK, -1))], axis=1).astype(bf)
    w_ukv = w_ukv_even[0].reshape(MLA_KV_RANK, MLA_HEADS, MLA_NOPE + MLA_V)
    w_ukv_p = jnp.concatenate([w_ukv[:, :, :MLA_NOPE].reshape(MLA_KV_RANK, -1),
                               w_ukv[:, :, MLA_NOPE:].reshape(MLA_KV_RANK, -1)], axis=1).astype(bf)
    w_io = w_in_odd[0]
    w_qk = jnp.concatenate([pair_layout(w_io[:, :MOBA_W]), pair_layout(w_io[:, MOBA_W: 2 * MOBA_W])],
                           axis=1).astype(bf)
    w_vg_t = w_io[:, 2 * MOBA_W:].T.astype(bf)

    h_main = _project_even(x0, w_main, cos_r, sin_r)
    ret_out = _retention(h_main)
    q_full, k_full, v_m = _mla_prep(x0, w_small, q_norm_even[0].reshape(1, -1), kv_norm_even[0].reshape(1, -1),
                                    w_uq_p, w_ukv_p, cos_p, sin_p)
    mla_out = _mla_attention(q_full, k_full, v_m, h_main)
    x1 = _out_ln([ret_out, mla_out], w_out_even[0].astype(bf), x0,
                 ln_g[0].reshape(1, -1), ln_b[0].reshape(1, -1), name="out_ln_even")

    h_qk, h_vg_t = _project_odd(x1, w_qk, w_vg_t, cos_p, sin_p)
    moba_out_t = _moba(h_qk, h_vg_t)
    x2 = _out_ln([moba_out_t], w_out_odd[0].astype(bf), x1, ln_g[1].reshape(1, -1), ln_b[1].reshape(1, -1),
                 name="out_ln_odd", feature_major=(True,))
    return x2.reshape(BATCH, SEQ, D_MODEL)
```

```python
import functools
import math

import numpy as np
import jax
import jax.numpy as jnp
from jax import lax
from jax.experimental import pallas as pl
from jax.experimental.pallas import tpu as pltpu

D_MODEL = 1024
BATCH = 8
SEQ = 2048
DEPTH = 2
TOKENS = BATCH * SEQ

RET_HEADS = 4
RET_DK = 256
RET_DV = 512
MLA_HEADS = 8
MLA_NOPE = 128
MLA_ROPE = 64
MLA_V = 128
MLA_Q_RANK = 256
MLA_KV_RANK = 256
MOBA_HEADS = 16
MOBA_DH = 64
MOBA_BLOCK = 256
MOBA_TOPK = 3
MOBA_NB = SEQ // MOBA_BLOCK
MOBA_LOOKAHEAD = 2
ATT_LOOKAHEAD = 3

ROPE_THETA = 10000.0
LN_EPS = 1e-5
RMS_EPS = 1e-6
NEG = -1e30
DN_ALPHA = (2.0 * DEPTH) ** 0.25
LOG2E = math.log2(math.e)

RET_W = RET_HEADS * RET_DV
MLA_W = MLA_HEADS * MLA_V
MOBA_W = MOBA_HEADS * MOBA_DH

LANES = 128
HALF = LANES // 2
BF16_SUBLANES = 16
VMEM_LIMIT = 48 * 1024 * 1024

MAIN_RQ, MAIN_RK, MAIN_RV, MAIN_RG = 0, 1024, 2048, 4096
MAIN_W = 6144
SMALL_W = MLA_Q_RANK + MLA_KV_RANK + LANES

PROJ_TM = 512
PROJ_TN = 1024
OUT_SUB = 256
RET_CHUNK = 256
RET_CHUNKS_PER_STEP = 2
ATT_TQ = 256
ATT_TK = 256
PREP_TM = 512
OUT_TM = 512
TAB_TM = 1024


def _cparams(sem):
    return pltpu.CompilerParams(dimension_semantics=sem, vmem_limit_bytes=VMEM_LIMIT)


def _silu(g):
    h = 0.5 * g
    return h + h * jnp.tanh(h)


def _tables_kernel(pos_ref, inv_r_ref, inv_p_ref, cr_ref, sr_ref, cp_ref, sp_ref):
    pos = pos_ref[...].astype(jnp.float32)
    ang_r = pos * inv_r_ref[...]
    cr_ref[...] = jnp.cos(ang_r)
    sr_ref[...] = jnp.sin(ang_r)
    ang_p = pos * inv_p_ref[...]
    lane = lax.broadcasted_iota(jnp.int32, ang_p.shape, 1)
    cp_ref[...] = jnp.cos(ang_p)
    sin_p = jnp.sin(ang_p)
    sp_ref[...] = jnp.where(lane < HALF, -sin_p, sin_p)


def _tables(pos, inv_r, inv_p):
    tab = jax.ShapeDtypeStruct((TOKENS, LANES), jnp.float32)
    row = pl.BlockSpec((TAB_TM, LANES), lambda i: (i, 0))
    cst = pl.BlockSpec((1, LANES), lambda i: (0, 0))
    return pl.pallas_call(
        _tables_kernel,
        grid=(TOKENS // TAB_TM,),
        in_specs=[pl.BlockSpec((TAB_TM, 1), lambda i: (i, 0)), cst, cst],
        out_specs=[row, row, row, row],
        out_shape=[tab, tab, tab, tab],
        compiler_params=_cparams(("parallel",)),
        name="rot_tables",
    )(pos, inv_r, inv_p)


def _resident(shape):
    return pl.BlockSpec(shape, lambda i: (0,) * len(shape), pipeline_mode=pl.Buffered(1))


def _proj_even_kernel(x_ref, w_ref, wgt_ref, cos_ref, sin_ref, o_ref, ogt_ref):
    xb = x_ref[...].astype(jnp.bfloat16)
    cos = cos_ref[...]
    sin = sin_ref[...]
    for c0 in range(0, MAIN_W, PROJ_TN):
        acc = jnp.dot(xb, w_ref[:, c0: c0 + PROJ_TN], preferred_element_type=jnp.float32)
        if c0 not in (MAIN_RQ, MAIN_RK):
            o_ref[:, c0: c0 + PROJ_TN] = acc.astype(o_ref.dtype)
            continue
        for h0 in range(0, PROJ_TN, RET_DK):
            x1 = acc[:, h0: h0 + LANES]
            x2 = acc[:, h0 + LANES: h0 + RET_DK]
            r1 = x1 * cos - x2 * sin
            r2 = x2 * cos + x1 * sin
            if c0 == MAIN_RK:
                r1 = r1 * RET_DK ** -0.5
                r2 = r2 * RET_DK ** -0.5
            o_ref[:, c0 + h0: c0 + h0 + LANES] = r1.astype(o_ref.dtype)
            o_ref[:, c0 + h0 + LANES: c0 + h0 + RET_DK] = r2.astype(o_ref.dtype)
    ogt_ref[...] = lax.dot_general(wgt_ref[...], xb, (((1,), (1,)), ((), ())),
                                   preferred_element_type=jnp.float32).astype(ogt_ref.dtype)


def _project_even(x, w, w_gate_t, cos, sin):
    tm = PROJ_TM
    row = lambda width: pl.BlockSpec((tm, width), lambda i: (i, 0))
    return pl.pallas_call(
        _proj_even_kernel,
        grid=(TOKENS // tm,),
        in_specs=[row(D_MODEL), _resident(w.shape), _resident(w_gate_t.shape), row(LANES), row(LANES)],
        out_specs=[row(MAIN_W), pl.BlockSpec((MLA_W, tm), lambda i: (0, i))],
        out_shape=[jax.ShapeDtypeStruct((TOKENS, MAIN_W), jnp.bfloat16),
                   jax.ShapeDtypeStruct((MLA_W, TOKENS), jnp.bfloat16)],
        compiler_params=_cparams(("parallel",)),
        name="proj_even",
    )(x, w, w_gate_t, cos, sin)


def _proj_odd_kernel(x_ref, wqk_ref, wvgt_ref, cos_ref, sin_ref, oqk_ref, ovgt_ref):
    xb = x_ref[...].astype(jnp.bfloat16)
    cos = cos_ref[...]
    sin = sin_ref[...]
    for c0 in range(0, 2 * MOBA_W, PROJ_TN):
        acc = jnp.dot(xb, wqk_ref[:, c0: c0 + PROJ_TN], preferred_element_type=jnp.float32)
        for g0 in range(0, PROJ_TN, LANES):
            xg = acc[:, g0: g0 + LANES]
            rot = xg * cos + pltpu.roll(xg, HALF, axis=1) * sin
            if c0 < MOBA_W:
                rot = rot * (MOBA_DH ** -0.5 * LOG2E)
            oqk_ref[:, c0 + g0: c0 + g0 + LANES] = rot.astype(oqk_ref.dtype)
    for r0 in range(0, 2 * MOBA_W, PROJ_TN):
        ovgt_ref[r0: r0 + PROJ_TN, :] = lax.dot_general(
            wvgt_ref[r0: r0 + PROJ_TN, :], xb, (((1,), (1,)), ((), ())),
            preferred_element_type=jnp.float32).astype(ovgt_ref.dtype)


def _project_odd(x, w_qk, w_vg_t, cos, sin):
    tm = PROJ_TM
    row = lambda width: pl.BlockSpec((tm, width), lambda i: (i, 0))
    return pl.pallas_call(
        _proj_odd_kernel,
        grid=(TOKENS // tm,),
        in_specs=[row(D_MODEL), _resident(w_qk.shape), _resident(w_vg_t.shape), row(LANES), row(LANES)],
        out_specs=[row(2 * MOBA_W), pl.BlockSpec((2 * MOBA_W, tm), lambda i: (0, i))],
        out_shape=[jax.ShapeDtypeStruct((TOKENS, 2 * MOBA_W), jnp.bfloat16),
                   jax.ShapeDtypeStruct((2 * MOBA_W, TOKENS), jnp.bfloat16)],
        compiler_params=_cparams(("parallel",)),
        name="proj_odd",
    )(x, w_qk, w_vg_t, cos, sin)


def _retention_kernel(q_ref, k_ref, v_ref, g_ref, o_ref, state_ref, decay_ref, xi_ref, zeta_ref, gch_ref):
    C = RET_CHUNK

    @pl.when((pl.program_id(1) == 0) & (pl.program_id(2) == 0))
    def _():
        def log_gamma(shape):
            hf = jnp.full(shape, pl.program_id(0), jnp.int32).astype(jnp.float32)
            return jnp.log(1.0 - jnp.exp2(-5.0 - hf))

        row = lax.broadcasted_iota(jnp.int32, (C, C), 0)
        col = lax.broadcasted_iota(jnp.int32, (C, C), 1)
        rel = (row - col).astype(jnp.float32)
        decay_ref[...] = jnp.where(rel >= 0, jnp.exp(log_gamma((C, C)) * jnp.maximum(rel, 0.0)), 0.0)
        idx = lax.broadcasted_iota(jnp.int32, (C, LANES), 0).astype(jnp.float32)
        xi_ref[...] = jnp.exp(log_gamma((C, LANES)) * (idx + 1.0))
        zeta_ref[...] = jnp.exp(log_gamma((C, LANES)) * (C - 1.0 - idx))
        gch_ref[...] = jnp.exp(log_gamma(gch_ref.shape) * float(C))

    @pl.when(pl.program_id(2) == 0)
    def _():
        state_ref[...] = jnp.zeros_like(state_ref)

    decay = decay_ref[...]
    xi = jnp.concatenate([xi_ref[...]] * (RET_DV // LANES), axis=1)
    zeta = jnp.concatenate([zeta_ref[...]] * (RET_DK // LANES), axis=1)
    g_chunk = gch_ref[0:1, :]

    outs = []
    for t in range(RET_CHUNKS_PER_STEP):
        rows = slice(t * C, (t + 1) * C)
        q = q_ref[rows, :]
        k = k_ref[rows, :]
        v = v_ref[rows, :]
        s = lax.dot_general(q, k, (((1,), (1,)), ((), ())), preferred_element_type=jnp.float32)
        state = state_ref[...]
        cross = jnp.dot(q, state.astype(jnp.bfloat16), preferred_element_type=jnp.float32)
        kz = (k.astype(jnp.float32) * zeta).astype(jnp.bfloat16)
        kv = lax.dot_general(kz, v, (((0,), (0,)), ((), ())), preferred_element_type=jnp.float32)
        state_ref[...] = state * g_chunk + kv
        inner = (s * decay).astype(jnp.bfloat16)
        outs.append(jnp.dot(inner, v, preferred_element_type=jnp.float32) + cross * xi)

    for t, o in enumerate(outs):
        rows = slice(t * C, (t + 1) * C)
        mu = jnp.mean(o, axis=-1, keepdims=True)
        oc = o - mu
        var = jnp.mean(oc * oc, axis=-1, keepdims=True)
        on = oc * lax.rsqrt(var + LN_EPS)
        o_ref[rows, :] = (on * _silu(g_ref[rows, :].astype(jnp.float32))).astype(o_ref.dtype)


def _retention(h_main):
    C = RET_CHUNK
    rows = C * RET_CHUNKS_PER_STEP
    ns = SEQ // rows
    return pl.pallas_call(
        _retention_kernel,
        grid=(RET_HEADS, BATCH, ns),
        in_specs=[
            pl.BlockSpec((rows, RET_DK), lambda h, b, c: (b * ns + c, MAIN_RQ // RET_DK + h)),
            pl.BlockSpec((rows, RET_DK), lambda h, b, c: (b * ns + c, MAIN_RK // RET_DK + h)),
            pl.BlockSpec((rows, RET_DV), lambda h, b, c: (b * ns + c, MAIN_RV // RET_DV + h)),
            pl.BlockSpec((rows, RET_DV), lambda h, b, c: (b * ns + c, MAIN_RG // RET_DV + h)),
        ],
        out_specs=pl.BlockSpec((rows, RET_DV), lambda h, b, c: (b * ns + c, h)),
        out_shape=jax.ShapeDtypeStruct((TOKENS, RET_W), jnp.bfloat16),
        scratch_shapes=[
            pltpu.VMEM((RET_DK, RET_DV), jnp.float32),
            pltpu.VMEM((C, C), jnp.float32),
            pltpu.VMEM((C, LANES), jnp.float32),
            pltpu.VMEM((C, LANES), jnp.float32),
            pltpu.VMEM((8, RET_DV), jnp.float32),
        ],
        compiler_params=_cparams(("arbitrary", "arbitrary", "arbitrary")),
        name="retention",
    )(h_main, h_main, h_main, h_main)


def _rms_norm(x, g):
    return x * lax.rsqrt(jnp.mean(x * x, axis=-1, keepdims=True) + RMS_EPS) * g


def _mla_prep_kernel(x_ref, ws_ref, qg_ref, kvg_ref, wuq_ref, wuk_ref, wuvt_ref, cos_ref, sin_ref,
                     q_ref, k_ref, vt_ref):
    xb = x_ref[...].astype(jnp.bfloat16)
    hs = jnp.dot(xb, ws_ref[...], preferred_element_type=jnp.float32)
    mq = hs[:, :MLA_Q_RANK]
    mkv = hs[:, MLA_Q_RANK: MLA_Q_RANK + MLA_KV_RANK]
    mkr = hs[:, MLA_Q_RANK + MLA_KV_RANK:]
    cos = cos_ref[...]
    sin = sin_ref[...]

    qn = _rms_norm(mq, qg_ref[...]).astype(jnp.bfloat16)
    kvn = _rms_norm(mkv, kvg_ref[...]).astype(jnp.bfloat16)
    qm = jnp.dot(qn, wuq_ref[...], preferred_element_type=jnp.float32)
    kv = jnp.dot(kvn, wuk_ref[...], preferred_element_type=jnp.float32)
    vt_ref[...] = lax.dot_general(wuvt_ref[...], kvn, (((1,), (1,)), ((), ())),
                                  preferred_element_type=jnp.float32).astype(vt_ref.dtype)

    kpe = (mkr * cos + pltpu.roll(mkr, HALF, axis=1) * sin).astype(k_ref.dtype)
    scale = (MLA_NOPE + MLA_ROPE) ** -0.5 * LOG2E
    lane = lax.broadcasted_iota(jnp.int32, (x_ref.shape[0], LANES), 1)
    second = (lane // (HALF // 2)) % 2
    nope_w = MLA_HEADS * MLA_NOPE
    for p in range(MLA_HEADS // 2):
        pe = qm[:, nope_w + p * LANES: nope_w + (p + 1) * LANES]
        pe = (pe * cos + pltpu.roll(pe, HALF, axis=1) * sin) * scale
        for e in range(2):
            h = 2 * p + e
            base = h * 2 * LANES
            q_ref[:, base: base + LANES] = (qm[:, h * LANES: (h + 1) * LANES] * scale).astype(q_ref.dtype)
            q_ref[:, base + LANES: base + 2 * LANES] = jnp.where(second == e, pe, 0.0).astype(q_ref.dtype)
            k_ref[:, base: base + LANES] = kv[:, h * LANES: (h + 1) * LANES].astype(k_ref.dtype)
            k_ref[:, base + LANES: base + 2 * LANES] = kpe


def _mla_prep(x, w_small, q_g, kv_g, w_uq, w_uk, w_uv_t, cos_p, sin_p):
    tm = PREP_TM
    full = lambda a: _resident(a.shape)
    qk_w = MLA_HEADS * 2 * LANES
    return pl.pallas_call(
        _mla_prep_kernel,
        grid=(TOKENS // tm,),
        in_specs=[
            pl.BlockSpec((tm, D_MODEL), lambda i: (i, 0)),
            full(w_small), full(q_g), full(kv_g), full(w_uq), full(w_uk), full(w_uv_t),
            pl.BlockSpec((tm, LANES), lambda i: (i, 0)),
            pl.BlockSpec((tm, LANES), lambda i: (i, 0)),
        ],
        out_specs=[
            pl.BlockSpec((tm, qk_w), lambda i: (i, 0)),
            pl.BlockSpec((tm, qk_w), lambda i: (i, 0)),
            pl.BlockSpec((MLA_W, tm), lambda i: (0, i)),
        ],
        out_shape=[
            jax.ShapeDtypeStruct((TOKENS, qk_w), jnp.bfloat16),
            jax.ShapeDtypeStruct((TOKENS, qk_w), jnp.bfloat16),
            jax.ShapeDtypeStruct((MLA_W, TOKENS), jnp.bfloat16),
        ],
        compiler_params=_cparams(("parallel",)),
        name="mla_prep",
    )(x, w_small, q_g, kv_g, w_uq, w_uk, w_uv_t, cos_p, sin_p)


def _mla_attn_kernel(q_ref, k_ref, vt_ref, gt_ref, o_ref):
    krow = lax.broadcasted_iota(jnp.int32, (ATT_TQ, ATT_TQ), 0)
    qcol = lax.broadcasted_iota(jnp.int32, (ATT_TQ, ATT_TQ), 1)
    vt_aug = jnp.concatenate([vt_ref[...], jnp.ones((BF16_SUBLANES, SEQ), jnp.bfloat16)], axis=0)

    def scores_softmax(i):
        r0 = i * ATT_TQ
        n = r0 + ATT_TQ
        st = lax.dot_general(k_ref[0:n, :], q_ref[r0:n, :], (((1,), (1,)), ((), ())),
                             preferred_element_type=jnp.float32)
        own = jnp.where(krow <= qcol, st[r0:, :], NEG)
        sm = own if i == 0 else jnp.concatenate([st[:r0, :], own], axis=0)
        m = jnp.max(sm, axis=0, keepdims=True)
        return jnp.exp2(sm - m).astype(jnp.bfloat16)

    def weighted_values(i, p):
        r0 = i * ATT_TQ
        n = r0 + ATT_TQ
        ot = jnp.dot(vt_aug[:, 0:n], p, preferred_element_type=jnp.float32)
        gate = _silu(gt_ref[:, r0:n].astype(jnp.float32))
        o_ref[:, r0:n] = (ot[:MLA_V] / ot[MLA_V: MLA_V + 1] * gate).astype(o_ref.dtype)

    pending = []
    for i in range(SEQ // ATT_TQ):
        pending.append((i, scores_softmax(i)))
        if len(pending) > ATT_LOOKAHEAD:
            weighted_values(*pending.pop(0))
    for item in pending:
        weighted_values(*item)


def _mla_attention(q_full, k_full, v_t, gate_t):
    qk_d = 2 * LANES
    feat = pl.BlockSpec((MLA_V, SEQ), lambda b, h: (h, b))
    return pl.pallas_call(
        _mla_attn_kernel,
        grid=(BATCH, MLA_HEADS),
        in_specs=[
            pl.BlockSpec((SEQ, qk_d), lambda b, h: (b, h)),
            pl.BlockSpec((SEQ, qk_d), lambda b, h: (b, h)),
            feat,
            feat,
        ],
        out_specs=feat,
        out_shape=jax.ShapeDtypeStruct((MLA_W, TOKENS), jnp.bfloat16),
        compiler_params=_cparams(("parallel", "parallel")),
        name="mla_attention",
    )(q_full, k_full, v_t, gate_t)


def _out_ln_kernel(*refs, feature_major):
    n_acts = len(feature_major)
    a_refs = refs[:n_acts]
    w_ref, x_ref, g_ref, b_ref, o_ref = refs[n_acts:]
    for r0 in range(0, o_ref.shape[0], OUT_SUB):
        y = None
        k0 = 0
        for a_ref, fm in zip(a_refs, feature_major):
            if fm:
                kw = a_ref.shape[0]
                t = lax.dot_general(a_ref[:, r0: r0 + OUT_SUB], w_ref[k0: k0 + kw, :], (((0,), (0,)), ((), ())),
                                    preferred_element_type=jnp.float32)
            else:
                kw = a_ref.shape[1]
                t = jnp.dot(a_ref[r0: r0 + OUT_SUB, :], w_ref[k0: k0 + kw, :], preferred_element_type=jnp.float32)
            y = t if y is None else y + t
            k0 += kw
        z = DN_ALPHA * x_ref[r0: r0 + OUT_SUB, :] + y
        mu = jnp.mean(z, axis=-1, keepdims=True)
        zc = z - mu
        var = jnp.mean(zc * zc, axis=-1, keepdims=True)
        o_ref[r0: r0 + OUT_SUB, :] = zc * lax.rsqrt(var + LN_EPS) * g_ref[...] + b_ref[...]


def _out_ln(acts, w, x, ln_g, ln_b, name, feature_major=None):
    tm = OUT_TM
    feature_major = tuple(feature_major or (False,) * len(acts))
    in_specs = [pl.BlockSpec((a.shape[0], tm), lambda i: (0, i)) if fm else
                pl.BlockSpec((tm, a.shape[1]), lambda i: (i, 0)) for a, fm in zip(acts, feature_major)]
    in_specs += [_resident(w.shape), pl.BlockSpec((tm, D_MODEL), lambda i: (i, 0)),
                 _resident(ln_g.shape), _resident(ln_b.shape)]
    return pl.pallas_call(
        functools.partial(_out_ln_kernel, feature_major=feature_major),
        grid=(TOKENS // tm,),
        in_specs=in_specs,
        out_specs=pl.BlockSpec((tm, D_MODEL), lambda i: (i, 0)),
        out_shape=jax.ShapeDtypeStruct((TOKENS, D_MODEL), jnp.float32),
        compiler_params=_cparams(("parallel",)),
        name=name,
    )(*acts, w, x, ln_g, ln_b)


def _split3(x):
    hi = x.astype(jnp.bfloat16)
    r1 = x - hi.astype(jnp.float32)
    mid = r1.astype(jnp.bfloat16)
    lo = (r1 - mid.astype(jnp.float32)).astype(jnp.bfloat16)
    return hi, mid, lo


def _moba_kernel(q_ref, k_ref, vt_ref, gt_ref, o_ref):
    L = MOBA_BLOCK
    NB = MOBA_NB
    means = [jnp.mean(k_ref[n * L: (n + 1) * L, :].astype(jnp.float32), axis=0, keepdims=True) for n in range(NB)]
    km = jnp.concatenate(means, axis=0)
    parts = [p.astype(jnp.float32) for p in _split3(km)]
    km3 = jnp.concatenate(parts + [jnp.zeros_like(km)], axis=0).astype(jnp.bfloat16)

    q = q_ref[...]
    lane = lax.broadcasted_iota(jnp.int32, (SEQ, LANES), 1)
    second = (lane // (HALF // 2)) % 2
    blk = lax.broadcasted_iota(jnp.int32, (NB, SEQ), 0)
    qblk = lax.broadcasted_iota(jnp.int32, (NB, SEQ), 1) // L
    past = blk < qblk
    krow = lax.broadcasted_iota(jnp.int32, (L, L), 0)
    qcol = lax.broadcasted_iota(jnp.int32, (L, L), 1)
    hd = MOBA_DH
    qes, sels = [], []
    for e in range(2):
        qe = jnp.where(second == e, q, jnp.zeros_like(q))
        g3 = lax.dot_general(km3, qe, (((1,), (1,)), ((), ())), preferred_element_type=jnp.float32)
        gate = jnp.where(past, g3[0:NB] + g3[NB: 2 * NB] + g3[2 * NB: 3 * NB], NEG)
        rank = jnp.zeros((NB, SEQ), jnp.float32)
        for n in range(NB - 1):
            gn = gate[n: n + 1, :]
            rank = rank + jnp.where(gn > gate, 1.0, jnp.where(gn == gate, jnp.where(blk > n, 1.0, 0.0), 0.0))
        qes.append(qe)
        sels.append(jnp.where(past, jnp.where(rank < float(MOBA_TOPK), 1.0, 0.0), 0.0))

    def scores_softmax(e, i):
        r0 = i * L
        n_keys = r0 + L
        st = lax.dot_general(k_ref[0:n_keys, :], qes[e][r0:n_keys, :], (((1,), (1,)), ((), ())),
                             preferred_element_type=jnp.float32)
        pieces = [jnp.where(sels[e][j: j + 1, r0:n_keys] > 0.0, st[j * L: (j + 1) * L, :], NEG) for j in range(i)]
        pieces.append(jnp.where(krow <= qcol, st[r0:, :], NEG))
        sm = pieces[0] if i == 0 else jnp.concatenate(pieces, axis=0)
        m = jnp.max(sm, axis=0, keepdims=True)
        return (jnp.exp2(sm - m).astype(jnp.bfloat16),)

    ones = jnp.ones((BF16_SUBLANES, SEQ), jnp.bfloat16)
    vt_aug = [jnp.concatenate([vt_ref[e * hd: (e + 1) * hd, :], ones], axis=0) for e in range(2)]

    def weighted_values(e, i, p):
        r0 = i * L
        n_keys = r0 + L
        ot = jnp.dot(vt_aug[e][:, 0:n_keys], p, preferred_element_type=jnp.float32)
        gate_t = _silu(gt_ref[e * hd: (e + 1) * hd, r0:n_keys].astype(jnp.float32))
        o_ref[e * hd: (e + 1) * hd, r0:n_keys] = (ot[:hd] / ot[hd: hd + 1] * gate_t).astype(o_ref.dtype)

    units = [(e, i) for i in reversed(range(NB)) for e in range(2)]
    pending = []
    for unit in units:
        pending.append(unit + scores_softmax(*unit))
        if len(pending) > MOBA_LOOKAHEAD:
            weighted_values(*pending.pop(0))
    for item in pending:
        weighted_values(*item)


def _moba(h_qk, h_vg_t):
    npair = MOBA_HEADS // 2
    return pl.pallas_call(
        _moba_kernel,
        grid=(BATCH, npair),
        in_specs=[
            pl.BlockSpec((SEQ, LANES), lambda b, p: (b, p)),
            pl.BlockSpec((SEQ, LANES), lambda b, p: (b, npair + p)),
            pl.BlockSpec((LANES, SEQ), lambda b, p: (p, b)),
            pl.BlockSpec((LANES, SEQ), lambda b, p: (npair + p, b)),
        ],
        out_specs=pl.BlockSpec((LANES, SEQ), lambda b, p: (p, b)),
        out_shape=jax.ShapeDtypeStruct((MOBA_W, TOKENS), jnp.bfloat16),
        compiler_params=_cparams(("parallel", "parallel")),
        name="moba_attention",
    )(h_qk, h_qk, h_vg_t, h_vg_t)


def kernel(x, positions, w_in_even, q_norm_even, w_uq_even, kv_norm_even, w_ukv_even, w_out_even, w_in_odd,
           w_out_odd, ln_g, ln_b):
    bf = jnp.bfloat16
    x0 = x.reshape(TOKENS, D_MODEL)
    pos = positions.reshape(TOKENS, 1)

    inv_r = (1.0 / (ROPE_THETA ** jnp.linspace(0.0, 1.0, RET_DK // 2, dtype=jnp.float32))).reshape(1, LANES)
    inv_rope = 1.0 / (ROPE_THETA ** (jnp.arange(0, MOBA_DH, 2, dtype=jnp.float32) / MOBA_DH))
    inv_p = jnp.tile(inv_rope, LANES // (MOBA_DH // 2)).reshape(1, LANES)
    cos_r, sin_r, cos_p, sin_p = _tables(pos, inv_r, inv_p)

    def permute_groups(w, src):
        k, n = w.shape
        width = len(src)
        sel = np.zeros((width, width), np.float32)
        sel[src, np.arange(width)] = 1.0
        return jnp.einsum("kgc,cd->kgd", w.reshape(k, n // width, width), sel).reshape(k, n)

    quarter = HALF // 2
    pair_src = np.concatenate([np.arange(0, quarter), np.arange(HALF, HALF + quarter),
                               np.arange(quarter, HALF), np.arange(HALF + quarter, LANES)])
    pair_layout = lambda w: permute_groups(w, pair_src)

    w_in = w_in_even[0]
    o = np.cumsum((0, 1024, 1024, RET_W, RET_W, MLA_Q_RANK, MLA_KV_RANK, MLA_ROPE, MLA_W))
    rot = permute_groups(w_in[:, :o[2]], np.concatenate([np.arange(0, RET_DK, 2), np.arange(1, RET_DK, 2)]))
    w_main = jnp.concatenate([rot, w_in[:, o[2]: o[4]]], axis=1).astype(bf)
    w_mg_t = w_in[:, o[7]: o[8]].T.astype(bf)
    mkr = w_in[:, o[6]: o[7]].reshape(D_MODEL, 2, 1, MLA_ROPE // 2)
    kr_dup = jnp.broadcast_to(mkr, (D_MODEL, 2, 2, MLA_ROPE // 2)).reshape(D_MODEL, LANES)
    w_small = jnp.concatenate([w_in[:, o[4]: o[6]], kr_dup], axis=1).astype(bf)
    w_uq = w_uq_even[0].reshape(MLA_Q_RANK, MLA_HEADS, MLA_NOPE + MLA_ROPE)
    w_uq_p = jnp.concatenate([w_uq[:, :, :MLA_NOPE].reshape(MLA_Q_RANK, -1),
                              pair_layout(w_uq[:, :, MLA_NOPE:].reshape(MLA_Q_RANK, -1))], axis=1).astype(bf)
    w_ukv = w_ukv_even[0].reshape(MLA_KV_RANK, MLA_HEADS, MLA_NOPE + MLA_V)
    w_uk = w_ukv[:, :, :MLA_NOPE].reshape(MLA_KV_RANK, -1).astype(bf)
    w_uv_t = w_ukv[:, :, MLA_NOPE:].reshape(MLA_KV_RANK, -1).T.astype(bf)
    w_io = w_in_odd[0]
    w_qk = jnp.concatenate([pair_layout(w_io[:, :MOBA_W]), pair_layout(w_io[:, MOBA_W: 2 * MOBA_W])],
                           axis=1).astype(bf)
    w_vg_t = w_io[:, 2 * MOBA_W:].T.astype(bf)

    h_main, mla_gate_t = _project_even(x0, w_main, w_mg_t, cos_r, sin_r)
    ret_out = _retention(h_main)
    q_full, k_full, v_t = _mla_prep(x0, w_small, q_norm_even[0].reshape(1, -1), kv_norm_even[0].reshape(1, -1),
                                    w_uq_p, w_uk, w_uv_t, cos_p, sin_p)
    mla_out_t = _mla_attention(q_full, k_full, v_t, mla_gate_t)
    x1 = _out_ln([ret_out, mla_out_t], w_out_even[0].astype(bf), x0,
                 ln_g[0].reshape(1, -1), ln_b[0].reshape(1, -1), name="out_ln_even", feature_major=(False, True))

    h_qk, h_vg_t = _project_odd(x1, w_qk, w_vg_t, cos_p, sin_p)
    moba_out_t = _moba(h_qk, h_vg_t)
    x2 = _out_ln([moba_out_t], w_out_odd[0].astype(bf), x1, ln_g[1].reshape(1, -1), ln_b[1].reshape(1, -1),
                 name="out_ln_odd", feature_major=(True,))
    return x2.reshape(BATCH, SEQ, D_MODEL)
```

```python
import functools
import math

import numpy as np
import jax
import jax.numpy as jnp
from jax import lax
from jax.experimental import pallas as pl
from jax.experimental.pallas import tpu as pltpu

D_MODEL = 1024
BATCH = 8
SEQ = 2048
DEPTH = 2
TOKENS = BATCH * SEQ

RET_HEADS = 4
RET_DK = 256
RET_DV = 512
MLA_HEADS = 8
MLA_NOPE = 128
MLA_ROPE = 64
MLA_V = 128
MLA_Q_RANK = 256
MLA_KV_RANK = 256
MOBA_HEADS = 16
MOBA_DH = 64
MOBA_BLOCK = 256
MOBA_TOPK = 3
MOBA_NB = SEQ // MOBA_BLOCK
MOBA_LOOKAHEAD = 2
ATT_LOOKAHEAD = 2

ROPE_THETA = 10000.0
LN_EPS = 1e-5
RMS_EPS = 1e-6
NEG = -1e30
DN_ALPHA = (2.0 * DEPTH) ** 0.25
LOG2E = math.log2(math.e)

RET_W = RET_HEADS * RET_DV
MLA_W = MLA_HEADS * MLA_V
MOBA_W = MOBA_HEADS * MOBA_DH

LANES = 128
HALF = LANES // 2
BF16_SUBLANES = 16
VMEM_LIMIT = 48 * 1024 * 1024

MAIN_RQ, MAIN_RK, MAIN_RV, MAIN_RG, MAIN_MG = 0, 1024, 2048, 4096, 6144
MAIN_W = 7168
SMALL_W = MLA_Q_RANK + MLA_KV_RANK + LANES

PROJ_TM = 512
PROJ_TN = 1024
OUT_SUB = 256
RET_CHUNK = 256
RET_CHUNKS_PER_STEP = 8
ATT_TQ = 256
ATT_TK = 256
PREP_TM = 512
OUT_TM = 512
TAB_TM = 1024


def _cparams(sem):
    return pltpu.CompilerParams(dimension_semantics=sem, vmem_limit_bytes=VMEM_LIMIT)


def _silu(g):
    h = 0.5 * g
    return h + h * jnp.tanh(h)


def _tables_kernel(pos_ref, inv_r_ref, inv_p_ref, cr_ref, sr_ref, cp_ref, sp_ref):
    pos = pos_ref[...].astype(jnp.float32)
    ang_r = pos * inv_r_ref[...]
    cr_ref[...] = jnp.cos(ang_r)
    sr_ref[...] = jnp.sin(ang_r)
    ang_p = pos * inv_p_ref[...]
    lane = lax.broadcasted_iota(jnp.int32, ang_p.shape, 1)
    cp_ref[...] = jnp.cos(ang_p)
    sin_p = jnp.sin(ang_p)
    sp_ref[...] = jnp.where(lane < HALF, -sin_p, sin_p)


def _tables(pos, inv_r, inv_p):
    tab = jax.ShapeDtypeStruct((TOKENS, LANES), jnp.float32)
    row = pl.BlockSpec((TAB_TM, LANES), lambda i: (i, 0))
    cst = pl.BlockSpec((1, LANES), lambda i: (0, 0))
    return pl.pallas_call(
        _tables_kernel,
        grid=(TOKENS // TAB_TM,),
        in_specs=[pl.BlockSpec((TAB_TM, 1), lambda i: (i, 0)), cst, cst],
        out_specs=[row, row, row, row],
        out_shape=[tab, tab, tab, tab],
        compiler_params=_cparams(("parallel",)),
        name="rot_tables",
    )(pos, inv_r, inv_p)


def _resident(shape):
    return pl.BlockSpec(shape, lambda i: (0,) * len(shape), pipeline_mode=pl.Buffered(1))


def _proj_even_kernel(x_ref, wqk_ref, wv_ref, wg_ref, wmg_ref, cos_ref, sin_ref, o_ref):
    xb = x_ref[...].astype(jnp.bfloat16)
    cos = cos_ref[...]
    sin = sin_ref[...]
    w_refs = {MAIN_RQ: wqk_ref, MAIN_RV: wv_ref, MAIN_RG: wg_ref, MAIN_MG: wmg_ref}
    for c0 in range(0, MAIN_W, PROJ_TN):
        base = max(b for b in w_refs if b <= c0)
        acc = jnp.dot(xb, w_refs[base][:, c0 - base: c0 - base + PROJ_TN], preferred_element_type=jnp.float32)
        if c0 not in (MAIN_RQ, MAIN_RK):
            o_ref[:, c0: c0 + PROJ_TN] = acc.astype(o_ref.dtype)
            continue
        for h0 in range(0, PROJ_TN, RET_DK):
            x1 = acc[:, h0: h0 + LANES]
            x2 = acc[:, h0 + LANES: h0 + RET_DK]
            r1 = x1 * cos - x2 * sin
            r2 = x2 * cos + x1 * sin
            if c0 == MAIN_RK:
                r1 = r1 * RET_DK ** -0.5
                r2 = r2 * RET_DK ** -0.5
            o_ref[:, c0 + h0: c0 + h0 + LANES] = r1.astype(o_ref.dtype)
            o_ref[:, c0 + h0 + LANES: c0 + h0 + RET_DK] = r2.astype(o_ref.dtype)


def _project_even(x, w_qk, w_in_bf, w_mg, cos, sin):
    tm = PROJ_TM
    row = lambda width: pl.BlockSpec((tm, width), lambda i: (i, 0))
    col_block = lambda j: pl.BlockSpec((D_MODEL, RET_W), lambda i: (0, j), pipeline_mode=pl.Buffered(1))
    return pl.pallas_call(
        _proj_even_kernel,
        grid=(TOKENS // tm,),
        in_specs=[row(D_MODEL), _resident(w_qk.shape), col_block(MAIN_RV // RET_W), col_block(MAIN_RG // RET_W),
                  _resident(w_mg.shape), row(LANES), row(LANES)],
        out_specs=row(MAIN_W),
        out_shape=jax.ShapeDtypeStruct((TOKENS, MAIN_W), jnp.bfloat16),
        compiler_params=_cparams(("parallel",)),
        name="proj_even",
    )(x, w_qk, w_in_bf, w_in_bf, w_mg, cos, sin)


def _proj_odd_kernel(x_ref, wqk_ref, wvgt_ref, cos_ref, sin_ref, oqk_ref, ovgt_ref):
    xb = x_ref[...].astype(jnp.bfloat16)
    cos = cos_ref[...]
    sin = sin_ref[...]
    for c0 in range(0, 2 * MOBA_W, PROJ_TN):
        acc = jnp.dot(xb, wqk_ref[:, c0: c0 + PROJ_TN], preferred_element_type=jnp.float32)
        for g0 in range(0, PROJ_TN, LANES):
            xg = acc[:, g0: g0 + LANES]
            rot = xg * cos + pltpu.roll(xg, HALF, axis=1) * sin
            if c0 < MOBA_W:
                rot = rot * (MOBA_DH ** -0.5 * LOG2E)
            oqk_ref[:, c0 + g0: c0 + g0 + LANES] = rot.astype(oqk_ref.dtype)
    for r0 in range(0, 2 * MOBA_W, PROJ_TN):
        ovgt_ref[r0: r0 + PROJ_TN, :] = lax.dot_general(
            wvgt_ref[r0: r0 + PROJ_TN, :], xb, (((1,), (1,)), ((), ())),
            preferred_element_type=jnp.float32).astype(ovgt_ref.dtype)


def _project_odd(x, w_qk, w_vg_t, cos, sin):
    tm = PROJ_TM
    row = lambda width: pl.BlockSpec((tm, width), lambda i: (i, 0))
    return pl.pallas_call(
        _proj_odd_kernel,
        grid=(TOKENS // tm,),
        in_specs=[row(D_MODEL), _resident(w_qk.shape), _resident(w_vg_t.shape), row(LANES), row(LANES)],
        out_specs=[row(2 * MOBA_W), pl.BlockSpec((2 * MOBA_W, tm), lambda i: (0, i))],
        out_shape=[jax.ShapeDtypeStruct((TOKENS, 2 * MOBA_W), jnp.bfloat16),
                   jax.ShapeDtypeStruct((2 * MOBA_W, TOKENS), jnp.bfloat16)],
        compiler_params=_cparams(("parallel",)),
        name="proj_odd",
    )(x, w_qk, w_vg_t, cos, sin)


def _retention_kernel(q_ref, k_ref, v_ref, g_ref, o_ref, state_ref, decay_ref, xi_ref, zeta_ref, gch_ref):
    C = RET_CHUNK

    @pl.when((pl.program_id(1) == 0) & (pl.program_id(2) == 0))
    def _():
        def log_gamma(shape):
            hf = jnp.full(shape, pl.program_id(0), jnp.int32).astype(jnp.float32)
            return jnp.log(1.0 - jnp.exp2(-5.0 - hf))

        row = lax.broadcasted_iota(jnp.int32, (C, C), 0)
        col = lax.broadcasted_iota(jnp.int32, (C, C), 1)
        rel = (row - col).astype(jnp.float32)
        decay_ref[...] = jnp.where(rel >= 0, jnp.exp(log_gamma((C, C)) * jnp.maximum(rel, 0.0)), 0.0)
        idx = lax.broadcasted_iota(jnp.int32, (C, LANES), 0).astype(jnp.float32)
        xi_ref[...] = jnp.exp(log_gamma((C, LANES)) * (idx + 1.0))
        zeta_ref[...] = jnp.exp(log_gamma((C, LANES)) * (C - 1.0 - idx))
        gch_ref[...] = jnp.exp(log_gamma(gch_ref.shape) * float(C))

    @pl.when(pl.program_id(2) == 0)
    def _():
        state_ref[...] = jnp.zeros_like(state_ref)

    decay = decay_ref[...]
    xi = jnp.concatenate([xi_ref[...]] * (RET_DV // LANES), axis=1)
    zeta = jnp.concatenate([zeta_ref[...]] * (RET_DK // LANES), axis=1)
    g_chunk = gch_ref[0:1, :]

    outs = []
    for t in range(RET_CHUNKS_PER_STEP):
        rows = slice(t * C, (t + 1) * C)
        q = q_ref[rows, :]
        k = k_ref[rows, :]
        v = v_ref[rows, :]
        s = lax.dot_general(q, k, (((1,), (1,)), ((), ())), preferred_element_type=jnp.float32)
        state = state_ref[...]
        cross = jnp.dot(q, state.astype(jnp.bfloat16), preferred_element_type=jnp.float32)
        kz = (k.astype(jnp.float32) * zeta).astype(jnp.bfloat16)
        kv = lax.dot_general(kz, v, (((0,), (0,)), ((), ())), preferred_element_type=jnp.float32)
        state_ref[...] = state * g_chunk + kv
        inner = (s * decay).astype(jnp.bfloat16)
        outs.append(jnp.dot(inner, v, preferred_element_type=jnp.float32) + cross * xi)

    for t, o in enumerate(outs):
        rows = slice(t * C, (t + 1) * C)
        mu = jnp.mean(o, axis=-1, keepdims=True)
        oc = o - mu
        var = jnp.mean(oc * oc, axis=-1, keepdims=True)
        on = oc * lax.rsqrt(var + LN_EPS)
        o_ref[rows, :] = (on * _silu(g_ref[rows, :].astype(jnp.float32))).astype(o_ref.dtype)


def _retention(h_main):
    C = RET_CHUNK
    rows = C * RET_CHUNKS_PER_STEP
    ns = SEQ // rows
    return pl.pallas_call(
        _retention_kernel,
        grid=(RET_HEADS, BATCH, ns),
        in_specs=[
            pl.BlockSpec((rows, RET_DK), lambda h, b, c: (b * ns + c, MAIN_RQ // RET_DK + h)),
            pl.BlockSpec((rows, RET_DK), lambda h, b, c: (b * ns + c, MAIN_RK // RET_DK + h)),
            pl.BlockSpec((rows, RET_DV), lambda h, b, c: (b * ns + c, MAIN_RV // RET_DV + h)),
            pl.BlockSpec((rows, RET_DV), lambda h, b, c: (b * ns + c, MAIN_RG // RET_DV + h)),
        ],
        out_specs=pl.BlockSpec((rows, RET_DV), lambda h, b, c: (b * ns + c, h)),
        out_shape=jax.ShapeDtypeStruct((TOKENS, RET_W), jnp.bfloat16),
        scratch_shapes=[
            pltpu.VMEM((RET_DK, RET_DV), jnp.float32),
            pltpu.VMEM((C, C), jnp.float32),
            pltpu.VMEM((C, LANES), jnp.float32),
            pltpu.VMEM((C, LANES), jnp.float32),
            pltpu.VMEM((8, RET_DV), jnp.float32),
        ],
        compiler_params=_cparams(("arbitrary", "arbitrary", "arbitrary")),
        name="retention",
    )(h_main, h_main, h_main, h_main)


def _rms_norm(x, g):
    return x * lax.rsqrt(jnp.mean(x * x, axis=-1, keepdims=True) + RMS_EPS) * g


def _mla_prep_kernel(x_ref, ws_ref, qg_ref, kvg_ref, wuq_ref, wukv_ref, cos_ref, sin_ref,
                     q_ref, k_ref, v_ref):
    xb = x_ref[...].astype(jnp.bfloat16)
    hs = jnp.dot(xb, ws_ref[...], preferred_element_type=jnp.float32)
    mq = hs[:, :MLA_Q_RANK]
    mkv = hs[:, MLA_Q_RANK: MLA_Q_RANK + MLA_KV_RANK]
    mkr = hs[:, MLA_Q_RANK + MLA_KV_RANK:]
    cos = cos_ref[...]
    sin = sin_ref[...]

    qn = _rms_norm(mq, qg_ref[...]).astype(jnp.bfloat16)
    kvn = _rms_norm(mkv, kvg_ref[...]).astype(jnp.bfloat16)
    qm = jnp.dot(qn, wuq_ref[...], preferred_element_type=jnp.float32)
    kv = jnp.dot(kvn, wukv_ref[...], preferred_element_type=jnp.float32)

    kpe = (mkr * cos + pltpu.roll(mkr, HALF, axis=1) * sin).astype(k_ref.dtype)
    scale = (MLA_NOPE + MLA_ROPE) ** -0.5 * LOG2E
    lane = lax.broadcasted_iota(jnp.int32, (x_ref.shape[0], LANES), 1)
    second = (lane // (HALF // 2)) % 2
    nope_w = MLA_HEADS * MLA_NOPE
    for p in range(MLA_HEADS // 2):
        pe = qm[:, nope_w + p * LANES: nope_w + (p + 1) * LANES]
        pe = (pe * cos + pltpu.roll(pe, HALF, axis=1) * sin) * scale
        for e in range(2):
            h = 2 * p + e
            base = h * 2 * LANES
            q_ref[:, base: base + LANES] = (qm[:, h * LANES: (h + 1) * LANES] * scale).astype(q_ref.dtype)
            q_ref[:, base + LANES: base + 2 * LANES] = jnp.where(second == e, pe, 0.0).astype(q_ref.dtype)
            k_ref[:, base: base + LANES] = kv[:, h * LANES: (h + 1) * LANES].astype(k_ref.dtype)
            k_ref[:, base + LANES: base + 2 * LANES] = kpe
    v_ref[...] = kv[:, nope_w:].astype(v_ref.dtype)


def _mla_prep(x, w_small, q_g, kv_g, w_uq, w_ukv, cos_p, sin_p):
    tm = PREP_TM
    full = lambda a: _resident(a.shape)
    qk_w = MLA_HEADS * 2 * LANES
    return pl.pallas_call(
        _mla_prep_kernel,
        grid=(TOKENS // tm,),
        in_specs=[
            pl.BlockSpec((tm, D_MODEL), lambda i: (i, 0)),
            full(w_small), full(q_g), full(kv_g), full(w_uq), full(w_ukv),
            pl.BlockSpec((tm, LANES), lambda i: (i, 0)),
            pl.BlockSpec((tm, LANES), lambda i: (i, 0)),
        ],
        out_specs=[
            pl.BlockSpec((tm, qk_w), lambda i: (i, 0)),
            pl.BlockSpec((tm, qk_w), lambda i: (i, 0)),
            pl.BlockSpec((tm, MLA_W), lambda i: (i, 0)),
        ],
        out_shape=[
            jax.ShapeDtypeStruct((TOKENS, qk_w), jnp.bfloat16),
            jax.ShapeDtypeStruct((TOKENS, qk_w), jnp.bfloat16),
            jax.ShapeDtypeStruct((TOKENS, MLA_W), jnp.bfloat16),
        ],
        compiler_params=_cparams(("parallel",)),
        name="mla_prep",
    )(x, w_small, q_g, kv_g, w_uq, w_ukv, cos_p, sin_p)


def _mla_attn_kernel(q_ref, k_ref, v_ref, g_ref, o_ref):
    row = lax.broadcasted_iota(jnp.int32, (ATT_TQ, ATT_TQ), 0)
    col = lax.broadcasted_iota(jnp.int32, (ATT_TQ, ATT_TQ), 1)
    v_aug = jnp.concatenate([v_ref[...], jnp.ones((SEQ, LANES), jnp.bfloat16)], axis=1)

    def scores_softmax(i):
        r0 = i * ATT_TQ
        n = r0 + ATT_TQ
        s = lax.dot_general(q_ref[r0:n, :], k_ref[0:n, :], (((1,), (1,)), ((), ())),
                            preferred_element_type=jnp.float32)
        diag = jnp.where(col <= row, s[:, r0:], NEG)
        s = diag if i == 0 else jnp.concatenate([s[:, :r0], diag], axis=1)
        m = jnp.max(s, axis=-1, keepdims=True)
        return jnp.exp2(s - m).astype(jnp.bfloat16)

    def weighted_values(i, p):
        r0 = i * ATT_TQ
        n = r0 + ATT_TQ
        acc = jnp.dot(p, v_aug[0:n, :], preferred_element_type=jnp.float32)
        gate = _silu(g_ref[r0:n, :].astype(jnp.float32))
        o_ref[r0:n, :] = (acc[:, :MLA_V] / acc[:, MLA_V:] * gate).astype(o_ref.dtype)

    pending = []
    for i in range(SEQ // ATT_TQ):
        pending.append((i, scores_softmax(i)))
        if len(pending) > ATT_LOOKAHEAD:
            weighted_values(*pending.pop(0))
    for item in pending:
        weighted_values(*item)


def _mla_attention(q_full, k_full, v_m, h_main):
    qk_d = 2 * LANES
    return pl.pallas_call(
        _mla_attn_kernel,
        grid=(BATCH, MLA_HEADS),
        in_specs=[
            pl.BlockSpec((SEQ, qk_d), lambda b, h: (b, h)),
            pl.BlockSpec((SEQ, qk_d), lambda b, h: (b, h)),
            pl.BlockSpec((SEQ, MLA_V), lambda b, h: (b, h)),
            pl.BlockSpec((SEQ, MLA_V), lambda b, h: (b, MAIN_MG // MLA_V + h)),
        ],
        out_specs=pl.BlockSpec((SEQ, MLA_V), lambda b, h: (b, h)),
        out_shape=jax.ShapeDtypeStruct((TOKENS, MLA_W), jnp.bfloat16),
        compiler_params=_cparams(("parallel", "parallel")),
        name="mla_attention",
    )(q_full, k_full, v_m, h_main)


def _out_ln_kernel(*refs, feature_major):
    n_acts = len(feature_major)
    a_refs = refs[:n_acts]
    w_ref, x_ref, g_ref, b_ref, o_ref = refs[n_acts:]
    for r0 in range(0, o_ref.shape[0], OUT_SUB):
        y = None
        k0 = 0
        for a_ref, fm in zip(a_refs, feature_major):
            if fm:
                kw = a_ref.shape[0]
                t = lax.dot_general(a_ref[:, r0: r0 + OUT_SUB], w_ref[k0: k0 + kw, :], (((0,), (0,)), ((), ())),
                                    preferred_element_type=jnp.float32)
            else:
                kw = a_ref.shape[1]
                t = jnp.dot(a_ref[r0: r0 + OUT_SUB, :], w_ref[k0: k0 + kw, :], preferred_element_type=jnp.float32)
            y = t if y is None else y + t
            k0 += kw
        z = DN_ALPHA * x_ref[r0: r0 + OUT_SUB, :] + y
        mu = jnp.mean(z, axis=-1, keepdims=True)
        zc = z - mu
        var = jnp.mean(zc * zc, axis=-1, keepdims=True)
        o_ref[r0: r0 + OUT_SUB, :] = zc * lax.rsqrt(var + LN_EPS) * g_ref[...] + b_ref[...]


def _out_ln(acts, w, x, ln_g, ln_b, name, feature_major=None, tm=OUT_TM):
    feature_major = tuple(feature_major or (False,) * len(acts))
    in_specs = [pl.BlockSpec((a.shape[0], tm), lambda i: (0, i)) if fm else
                pl.BlockSpec((tm, a.shape[1]), lambda i: (i, 0)) for a, fm in zip(acts, feature_major)]
    in_specs += [_resident(w.shape), pl.BlockSpec((tm, D_MODEL), lambda i: (i, 0)),
                 _resident(ln_g.shape), _resident(ln_b.shape)]
    return pl.pallas_call(
        functools.partial(_out_ln_kernel, feature_major=feature_major),
        grid=(TOKENS // tm,),
        in_specs=in_specs,
        out_specs=pl.BlockSpec((tm, D_MODEL), lambda i: (i, 0)),
        out_shape=jax.ShapeDtypeStruct((TOKENS, D_MODEL), jnp.float32),
        compiler_params=_cparams(("parallel",)),
        name=name,
    )(*acts, w, x, ln_g, ln_b)


def _split3(x):
    hi = x.astype(jnp.bfloat16)
    r1 = x - hi.astype(jnp.float32)
    mid = r1.astype(jnp.bfloat16)
    lo = (r1 - mid.astype(jnp.float32)).astype(jnp.bfloat16)
    return hi, mid, lo


def _moba_kernel(q_ref, k_ref, vt_ref, gt_ref, o_ref):
    L = MOBA_BLOCK
    NB = MOBA_NB
    means = [jnp.mean(k_ref[n * L: (n + 1) * L, :].astype(jnp.float32), axis=0, keepdims=True) for n in range(NB)]
    km = jnp.concatenate(means, axis=0)
    parts = [p.astype(jnp.float32) for p in _split3(km)]
    km3 = jnp.concatenate(parts + [jnp.zeros_like(km)], axis=0).astype(jnp.bfloat16)

    q = q_ref[...]
    lane = lax.broadcasted_iota(jnp.int32, (SEQ, LANES), 1)
    second = (lane // (HALF // 2)) % 2
    blk = lax.broadcasted_iota(jnp.int32, (NB, SEQ), 0)
    qblk = lax.broadcasted_iota(jnp.int32, (NB, SEQ), 1) // L
    past = blk < qblk
    krow = lax.broadcasted_iota(jnp.int32, (L, L), 0)
    qcol = lax.broadcasted_iota(jnp.int32, (L, L), 1)
    hd = MOBA_DH
    qes, sels = [], []
    for e in range(2):
        qe = jnp.where(second == e, q, jnp.zeros_like(q))
        g3 = lax.dot_general(km3, qe, (((1,), (1,)), ((), ())), preferred_element_type=jnp.float32)
        gate = jnp.where(past, g3[0:NB] + g3[NB: 2 * NB] + g3[2 * NB: 3 * NB], NEG)
        rank = jnp.zeros((NB, SEQ), jnp.float32)
        for n in range(NB - 1):
            gn = gate[n: n + 1, :]
            rank = rank + jnp.where(gn > gate, 1.0, jnp.where(gn == gate, jnp.where(blk > n, 1.0, 0.0), 0.0))
        qes.append(qe)
        sels.append(jnp.where(past, jnp.where(rank < float(MOBA_TOPK), 1.0, 0.0), 0.0))

    def scores_softmax(e, i):
        r0 = i * L
        n_keys = r0 + L
        st = lax.dot_general(k_ref[0:n_keys, :], qes[e][r0:n_keys, :], (((1,), (1,)), ((), ())),
                             preferred_element_type=jnp.float32)
        pieces = [jnp.where(sels[e][j: j + 1, r0:n_keys] > 0.0, st[j * L: (j + 1) * L, :], NEG) for j in range(i)]
        pieces.append(jnp.where(krow <= qcol, st[r0:, :], NEG))
        sm = pieces[0] if i == 0 else jnp.concatenate(pieces, axis=0)
        m = jnp.max(sm, axis=0, keepdims=True)
        return (jnp.exp2(sm - m).astype(jnp.bfloat16),)

    ones = jnp.ones((BF16_SUBLANES, SEQ), jnp.bfloat16)
    vt_aug = [jnp.concatenate([vt_ref[e * hd: (e + 1) * hd, :], ones], axis=0) for e in range(2)]

    def weighted_values(e, i, p):
        r0 = i * L
        n_keys = r0 + L
        ot = jnp.dot(vt_aug[e][:, 0:n_keys], p, preferred_element_type=jnp.float32)
        gate_t = _silu(gt_ref[e * hd: (e + 1) * hd, r0:n_keys].astype(jnp.float32))
        o_ref[e * hd: (e + 1) * hd, r0:n_keys] = (ot[:hd] / ot[hd: hd + 1] * gate_t).astype(o_ref.dtype)

    units = [(e, i) for i in reversed(range(NB)) for e in range(2)]
    pending = []
    for unit in units:
        pending.append(unit + scores_softmax(*unit))
        if len(pending) > MOBA_LOOKAHEAD:
            weighted_values(*pending.pop(0))
    for item in pending:
        weighted_values(*item)


def _moba(h_qk, h_vg_t):
    npair = MOBA_HEADS // 2
    return pl.pallas_call(
        _moba_kernel,
        grid=(BATCH, npair),
        in_specs=[
            pl.BlockSpec((SEQ, LANES), lambda b, p: (b, p)),
            pl.BlockSpec((SEQ, LANES), lambda b, p: (b, npair + p)),
            pl.BlockSpec((LANES, SEQ), lambda b, p: (p, b)),
            pl.BlockSpec((LANES, SEQ), lambda b, p: (npair + p, b)),
        ],
        out_specs=pl.BlockSpec((LANES, SEQ), lambda b, p: (p, b)),
        out_shape=jax.ShapeDtypeStruct((MOBA_W, TOKENS), jnp.bfloat16),
        compiler_params=_cparams(("parallel", "parallel")),
        name="moba_attention",
    )(h_qk, h_qk, h_vg_t, h_vg_t)


def kernel(x, positions, w_in_even, q_norm_even, w_uq_even, kv_norm_even, w_ukv_even, w_out_even, w_in_odd,
           w_out_odd, ln_g, ln_b):
    bf = jnp.bfloat16
    x0 = x.reshape(TOKENS, D_MODEL)
    pos = positions.reshape(TOKENS, 1)

    inv_r = (1.0 / (ROPE_THETA ** jnp.linspace(0.0, 1.0, RET_DK // 2, dtype=jnp.float32))).reshape(1, LANES)
    inv_rope = 1.0 / (ROPE_THETA ** (jnp.arange(0, MOBA_DH, 2, dtype=jnp.float32) / MOBA_DH))
    inv_p = jnp.tile(inv_rope, LANES // (MOBA_DH // 2)).reshape(1, LANES)
    cos_r, sin_r, cos_p, sin_p = _tables(pos, inv_r, inv_p)

    def permute_groups(w, src):
        k, n = w.shape
        width = len(src)
        sel = np.zeros((width, width), np.float32)
        sel[src, np.arange(width)] = 1.0
        return jnp.einsum("kgc,cd->kgd", w.reshape(k, n // width, width), sel).reshape(k, n)

    quarter = HALF // 2
    pair_src = np.concatenate([np.arange(0, quarter), np.arange(HALF, HALF + quarter),
                               np.arange(quarter, HALF), np.arange(HALF + quarter, LANES)])
    pair_layout = lambda w: permute_groups(w, pair_src)

    w_in = w_in_even[0]
    o = np.cumsum((0, 1024, 1024, RET_W, RET_W, MLA_Q_RANK, MLA_KV_RANK, MLA_ROPE, MLA_W))
    w_rqk = permute_groups(w_in[:, :o[2]],
                           np.concatenate([np.arange(0, RET_DK, 2), np.arange(1, RET_DK, 2)])).astype(bf)
    w_in_bf = w_in.astype(bf)
    w_mg = w_in_bf[:, o[7]: o[8]]
    mkr = w_in_bf[:, o[6]: o[7]].reshape(D_MODEL, 2, 1, MLA_ROPE // 2)
    kr_dup = jnp.broadcast_to(mkr, (D_MODEL, 2, 2, MLA_ROPE // 2)).reshape(D_MODEL, LANES)
    w_small = jnp.concatenate([w_in_bf[:, o[4]: o[6]], kr_dup], axis=1)
    w_uq = w_uq_even[0].reshape(MLA_Q_RANK, MLA_HEADS, MLA_NOPE + MLA_ROPE)
    w_uq_p = jnp.concatenate([w_uq[:, :, :MLA_NOPE].reshape(MLA_Q_RANK, -1),
                              pair_layout(w_uq[:, :, MLA_NOPE:].reshape(MLA_Q_RANK, -1))], axis=1).astype(bf)
    w_ukv = w_ukv_even[0].reshape(MLA_KV_RANK, MLA_HEADS, MLA_NOPE + MLA_V)
    w_ukv_p = jnp.concatenate([w_ukv[:, :, :MLA_NOPE].reshape(MLA_KV_RANK, -1),
                               w_ukv[:, :, MLA_NOPE:].reshape(MLA_KV_RANK, -1)], axis=1).astype(bf)
    w_io = w_in_odd[0]
    w_qk = jnp.concatenate([pair_layout(w_io[:, :MOBA_W]), pair_layout(w_io[:, MOBA_W: 2 * MOBA_W])],
                           axis=1).astype(bf)
    w_vg_t = w_io[:, 2 * MOBA_W:].T.astype(bf)

    h_main = _project_even(x0, w_rqk, w_in_bf, w_mg, cos_r, sin_r)
    ret_out = _retention(h_main)
    q_full, k_full, v_m = _mla_prep(x0, w_small, q_norm_even[0].reshape(1, -1), kv_norm_even[0].reshape(1, -1),
                                    w_uq_p, w_ukv_p, cos_p, sin_p)
    mla_out = _mla_attention(q_full, k_full, v_m, h_main)
    x1 = _out_ln([ret_out, mla_out], w_out_even[0].astype(bf), x0,
                 ln_g[0].reshape(1, -1), ln_b[0].reshape(1, -1), name="out_ln_even")

    h_qk, h_vg_t = _project_odd(x1, w_qk, w_vg_t, cos_p, sin_p)
    moba_out_t = _moba(h_qk, h_vg_t)
    x2 = _out_ln([moba_out_t], w_out_odd[0].astype(bf), x1, ln_g[1].reshape(1, -1), ln_b[1].reshape(1, -1),
                 name="out_ln_odd", feature_major=(True,), tm=2 * OUT_TM)
    return x2.reshape(BATCH, SEQ, D_MODEL)
```

```python
import functools
import math

import numpy as np
import jax
import jax.numpy as jnp
from jax import lax
from jax.experimental import pallas as pl
from jax.experimental.pallas import tpu as pltpu

D_MODEL = 1024
BATCH = 8
SEQ = 2048
DEPTH = 2
TOKENS = BATCH * SEQ

RET_HEADS = 4
RET_DK = 256
RET_DV = 512
MLA_HEADS = 8
MLA_NOPE = 128
MLA_ROPE = 64
MLA_V = 128
MLA_Q_RANK = 256
MLA_KV_RANK = 256
MOBA_HEADS = 16
MOBA_DH = 64
MOBA_BLOCK = 256
MOBA_TOPK = 3
MOBA_NB = SEQ // MOBA_BLOCK
MOBA_LOOKAHEAD = 4
ATT_LOOKAHEAD = 3

ROPE_THETA = 10000.0
LN_EPS = 1e-5
RMS_EPS = 1e-6
NEG = -1e30
DN_ALPHA = (2.0 * DEPTH) ** 0.25
LOG2E = math.log2(math.e)

RET_W = RET_HEADS * RET_DV
MLA_W = MLA_HEADS * MLA_V
MOBA_W = MOBA_HEADS * MOBA_DH

LANES = 128
HALF = LANES // 2
BF16_SUBLANES = 16
VMEM_LIMIT = 48 * 1024 * 1024

MAIN_RQ, MAIN_RK, MAIN_RV, MAIN_RG, MAIN_MG = 0, 1024, 2048, 4096, 6144
MAIN_W = 7168
SMALL_W = MLA_Q_RANK + MLA_KV_RANK + LANES

PROJ_TM = 512
PROJ_TN = 1024
OUT_SUB = 256
RET_CHUNK = 256
RET_CHUNKS_PER_STEP = 8
ATT_TQ = 256
ATT_TK = 256
PREP_TM = 512
OUT_TM = 512
TAB_TM = 1024


def _cparams(sem):
    return pltpu.CompilerParams(dimension_semantics=sem, vmem_limit_bytes=VMEM_LIMIT)


def _silu(g):
    h = 0.5 * g
    return h + h * jnp.tanh(h)


def _tables_kernel(pos_ref, inv_r_ref, inv_p_ref, cr_ref, sr_ref, cp_ref, sp_ref):
    pos = pos_ref[...].astype(jnp.float32)
    ang_r = pos * inv_r_ref[...]
    cr_ref[...] = jnp.cos(ang_r)
    sr_ref[...] = jnp.sin(ang_r)
    ang_p = pos * inv_p_ref[...]
    lane = lax.broadcasted_iota(jnp.int32, ang_p.shape, 1)
    cp_ref[...] = jnp.cos(ang_p)
    sin_p = jnp.sin(ang_p)
    sp_ref[...] = jnp.where(lane < HALF, -sin_p, sin_p)


def _tables(pos, inv_r, inv_p):
    tab = jax.ShapeDtypeStruct((TOKENS, LANES), jnp.float32)
    row = pl.BlockSpec((TAB_TM, LANES), lambda i: (i, 0))
    cst = pl.BlockSpec((1, LANES), lambda i: (0, 0))
    return pl.pallas_call(
        _tables_kernel,
        grid=(TOKENS // TAB_TM,),
        in_specs=[pl.BlockSpec((TAB_TM, 1), lambda i: (i, 0)), cst, cst],
        out_specs=[row, row, row, row],
        out_shape=[tab, tab, tab, tab],
        compiler_params=_cparams(("parallel",)),
        name="rot_tables",
    )(pos, inv_r, inv_p)


def _resident(shape):
    return pl.BlockSpec(shape, lambda i: (0,) * len(shape), pipeline_mode=pl.Buffered(1))


def _proj_even_kernel(x_ref, wqk_ref, wv_ref, wg_ref, wmg_ref, cos_ref, sin_ref, o_ref):
    xb = x_ref[...].astype(jnp.bfloat16)
    cos = cos_ref[...]
    sin = sin_ref[...]
    w_refs = {MAIN_RQ: wqk_ref, MAIN_RV: wv_ref, MAIN_RG: wg_ref, MAIN_MG: wmg_ref}
    for c0 in range(0, MAIN_W, PROJ_TN):
        base = max(b for b in w_refs if b <= c0)
        acc = jnp.dot(xb, w_refs[base][:, c0 - base: c0 - base + PROJ_TN], preferred_element_type=jnp.float32)
        if c0 not in (MAIN_RQ, MAIN_RK):
            o_ref[:, c0: c0 + PROJ_TN] = acc.astype(o_ref.dtype)
            continue
        for h0 in range(0, PROJ_TN, RET_DK):
            x1 = acc[:, h0: h0 + LANES]
            x2 = acc[:, h0 + LANES: h0 + RET_DK]
            r1 = x1 * cos - x2 * sin
            r2 = x2 * cos + x1 * sin
            if c0 == MAIN_RK:
                r1 = r1 * RET_DK ** -0.5
                r2 = r2 * RET_DK ** -0.5
            o_ref[:, c0 + h0: c0 + h0 + LANES] = r1.astype(o_ref.dtype)
            o_ref[:, c0 + h0 + LANES: c0 + h0 + RET_DK] = r2.astype(o_ref.dtype)


def _project_even(x, w_qk, w_in_bf, w_mg, cos, sin):
    tm = PROJ_TM
    row = lambda width: pl.BlockSpec((tm, width), lambda i: (i, 0))
    col_block = lambda j: pl.BlockSpec((D_MODEL, RET_W), lambda i: (0, j), pipeline_mode=pl.Buffered(1))
    return pl.pallas_call(
        _proj_even_kernel,
        grid=(TOKENS // tm,),
        in_specs=[row(D_MODEL), _resident(w_qk.shape), col_block(MAIN_RV // RET_W), col_block(MAIN_RG // RET_W),
                  _resident(w_mg.shape), row(LANES), row(LANES)],
        out_specs=row(MAIN_W),
        out_shape=jax.ShapeDtypeStruct((TOKENS, MAIN_W), jnp.bfloat16),
        compiler_params=_cparams(("parallel",)),
        name="proj_even",
    )(x, w_qk, w_in_bf, w_in_bf, w_mg, cos, sin)


def _proj_odd_kernel(x_ref, wqk_ref, wvgt_ref, cos_ref, sin_ref, oqk_ref, ovgt_ref):
    xb = x_ref[...].astype(jnp.bfloat16)
    cos = cos_ref[...]
    sin = sin_ref[...]
    for c0 in range(0, 2 * MOBA_W, PROJ_TN):
        acc = jnp.dot(xb, wqk_ref[:, c0: c0 + PROJ_TN], preferred_element_type=jnp.float32)
        for g0 in range(0, PROJ_TN, LANES):
            xg = acc[:, g0: g0 + LANES]
            rot = xg * cos + pltpu.roll(xg, HALF, axis=1) * sin
            if c0 < MOBA_W:
                rot = rot * (MOBA_DH ** -0.5 * LOG2E)
            oqk_ref[:, c0 + g0: c0 + g0 + LANES] = rot.astype(oqk_ref.dtype)
    for r0 in range(0, 2 * MOBA_W, PROJ_TN):
        ovgt_ref[r0: r0 + PROJ_TN, :] = lax.dot_general(
            wvgt_ref[r0: r0 + PROJ_TN, :], xb, (((1,), (1,)), ((), ())),
            preferred_element_type=jnp.float32).astype(ovgt_ref.dtype)


def _project_odd(x, w_qk, w_vg_t, cos, sin):
    tm = PROJ_TM
    row = lambda width: pl.BlockSpec((tm, width), lambda i: (i, 0))
    return pl.pallas_call(
        _proj_odd_kernel,
        grid=(TOKENS // tm,),
        in_specs=[row(D_MODEL), _resident(w_qk.shape), _resident(w_vg_t.shape), row(LANES), row(LANES)],
        out_specs=[row(2 * MOBA_W), pl.BlockSpec((2 * MOBA_W, tm), lambda i: (0, i))],
        out_shape=[jax.ShapeDtypeStruct((TOKENS, 2 * MOBA_W), jnp.bfloat16),
                   jax.ShapeDtypeStruct((2 * MOBA_W, TOKENS), jnp.bfloat16)],
        compiler_params=_cparams(("parallel",)),
        name="proj_odd",
    )(x, w_qk, w_vg_t, cos, sin)


def _retention_kernel(q_ref, k_ref, v_ref, g_ref, o_ref, state_ref, decay_ref, xi_ref, zeta_ref, gch_ref):
    C = RET_CHUNK

    @pl.when((pl.program_id(1) == 0) & (pl.program_id(2) == 0))
    def _():
        def log_gamma(shape):
            hf = jnp.full(shape, pl.program_id(0), jnp.int32).astype(jnp.float32)
            return jnp.log(1.0 - jnp.exp2(-5.0 - hf))

        row = lax.broadcasted_iota(jnp.int32, (C, C), 0)
        col = lax.broadcasted_iota(jnp.int32, (C, C), 1)
        rel = (row - col).astype(jnp.float32)
        decay_ref[...] = jnp.where(rel >= 0, jnp.exp(log_gamma((C, C)) * jnp.maximum(rel, 0.0)), 0.0)
        idx = lax.broadcasted_iota(jnp.int32, (C, LANES), 0).astype(jnp.float32)
        xi_ref[...] = jnp.exp(log_gamma((C, LANES)) * (idx + 1.0))
        zeta_ref[...] = jnp.exp(log_gamma((C, LANES)) * (C - 1.0 - idx))
        gch_ref[...] = jnp.exp(log_gamma(gch_ref.shape) * float(C))

    @pl.when(pl.program_id(2) == 0)
    def _():
        state_ref[...] = jnp.zeros_like(state_ref)

    decay = decay_ref[...]
    xi = jnp.concatenate([xi_ref[...]] * (RET_DV // LANES), axis=1)
    zeta = jnp.concatenate([zeta_ref[...]] * (RET_DK // LANES), axis=1)
    g_chunk = gch_ref[0:1, :]

    outs = []
    for t in range(RET_CHUNKS_PER_STEP):
        rows = slice(t * C, (t + 1) * C)
        q = q_ref[rows, :]
        k = k_ref[rows, :]
        v = v_ref[rows, :]
        s = lax.dot_general(q, k, (((1,), (1,)), ((), ())), preferred_element_type=jnp.float32)
        state = state_ref[...]
        cross = jnp.dot(q, state.astype(jnp.bfloat16), preferred_element_type=jnp.float32)
        kz = (k.astype(jnp.float32) * zeta).astype(jnp.bfloat16)
        kv = lax.dot_general(kz, v, (((0,), (0,)), ((), ())), preferred_element_type=jnp.float32)
        state_ref[...] = state * g_chunk + kv
        inner = (s * decay).astype(jnp.bfloat16)
        outs.append(jnp.dot(inner, v, preferred_element_type=jnp.float32) + cross * xi)

    for t, o in enumerate(outs):
        rows = slice(t * C, (t + 1) * C)
        mu = jnp.mean(o, axis=-1, keepdims=True)
        oc = o - mu
        var = jnp.mean(oc * oc, axis=-1, keepdims=True)
        on = oc * lax.rsqrt(var + LN_EPS)
        o_ref[rows, :] = (on * _silu(g_ref[rows, :].astype(jnp.float32))).astype(o_ref.dtype)


def _retention(h_main):
    C = RET_CHUNK
    rows = C * RET_CHUNKS_PER_STEP
    ns = SEQ // rows
    return pl.pallas_call(
        _retention_kernel,
        grid=(RET_HEADS, BATCH, ns),
        in_specs=[
            pl.BlockSpec((rows, RET_DK), lambda h, b, c: (b * ns + c, MAIN_RQ // RET_DK + h)),
            pl.BlockSpec((rows, RET_DK), lambda h, b, c: (b * ns + c, MAIN_RK // RET_DK + h)),
            pl.BlockSpec((rows, RET_DV), lambda h, b, c: (b * ns + c, MAIN_RV // RET_DV + h)),
            pl.BlockSpec((rows, RET_DV), lambda h, b, c: (b * ns + c, MAIN_RG // RET_DV + h)),
        ],
        out_specs=pl.BlockSpec((rows, RET_DV), lambda h, b, c: (b * ns + c, h)),
        out_shape=jax.ShapeDtypeStruct((TOKENS, RET_W), jnp.bfloat16),
        scratch_shapes=[
            pltpu.VMEM((RET_DK, RET_DV), jnp.float32),
            pltpu.VMEM((C, C), jnp.float32),
            pltpu.VMEM((C, LANES), jnp.float32),
            pltpu.VMEM((C, LANES), jnp.float32),
            pltpu.VMEM((8, RET_DV), jnp.float32),
        ],
        compiler_params=_cparams(("arbitrary", "arbitrary", "arbitrary")),
        name="retention",
    )(h_main, h_main, h_main, h_main)


def _rms_norm(x, g):
    return x * lax.rsqrt(jnp.mean(x * x, axis=-1, keepdims=True) + RMS_EPS) * g


def _mla_prep_kernel(x_ref, ws_ref, qg_ref, kvg_ref, wuq_ref, wukv_ref, cos_ref, sin_ref,
                     q_ref, k_ref, v_ref):
    xb = x_ref[...].astype(jnp.bfloat16)
    hs = jnp.dot(xb, ws_ref[...], preferred_element_type=jnp.float32)
    mq = hs[:, :MLA_Q_RANK]
    mkv = hs[:, MLA_Q_RANK: MLA_Q_RANK + MLA_KV_RANK]
    mkr = hs[:, MLA_Q_RANK + MLA_KV_RANK:]
    cos = cos_ref[...]
    sin = sin_ref[...]

    qn = _rms_norm(mq, qg_ref[...]).astype(jnp.bfloat16)
    kvn = _rms_norm(mkv, kvg_ref[...]).astype(jnp.bfloat16)
    qm = jnp.dot(qn, wuq_ref[...], preferred_element_type=jnp.float32)
    kv = jnp.dot(kvn, wukv_ref[...], preferred_element_type=jnp.float32)

    kpe = (mkr * cos + pltpu.roll(mkr, HALF, axis=1) * sin).astype(k_ref.dtype)
    scale = (MLA_NOPE + MLA_ROPE) ** -0.5 * LOG2E
    lane = lax.broadcasted_iota(jnp.int32, (x_ref.shape[0], LANES), 1)
    second = (lane // (HALF // 2)) % 2
    nope_w = MLA_HEADS * MLA_NOPE
    for p in range(MLA_HEADS // 2):
        pe = qm[:, nope_w + p * LANES: nope_w + (p + 1) * LANES]
        pe = (pe * cos + pltpu.roll(pe, HALF, axis=1) * sin) * scale
        for e in range(2):
            h = 2 * p + e
            base = h * 2 * LANES
            q_ref[:, base: base + LANES] = (qm[:, h * LANES: (h + 1) * LANES] * scale).astype(q_ref.dtype)
            q_ref[:, base + LANES: base + 2 * LANES] = jnp.where(second == e, pe, 0.0).astype(q_ref.dtype)
            k_ref[:, base: base + LANES] = kv[:, h * LANES: (h + 1) * LANES].astype(k_ref.dtype)
            k_ref[:, base + LANES: base + 2 * LANES] = kpe
    v_ref[...] = kv[:, nope_w:].astype(v_ref.dtype)


def _mla_prep(x, w_small, q_g, kv_g, w_uq, w_ukv, cos_p, sin_p):
    tm = PREP_TM
    full = lambda a: _resident(a.shape)
    qk_w = MLA_HEADS * 2 * LANES
    return pl.pallas_call(
        _mla_prep_kernel,
        grid=(TOKENS // tm,),
        in_specs=[
            pl.BlockSpec((tm, D_MODEL), lambda i: (i, 0)),
            full(w_small), full(q_g), full(kv_g), full(w_uq), full(w_ukv),
            pl.BlockSpec((tm, LANES), lambda i: (i, 0)),
            pl.BlockSpec((tm, LANES), lambda i: (i, 0)),
        ],
        out_specs=[
            pl.BlockSpec((tm, qk_w), lambda i: (i, 0)),
            pl.BlockSpec((tm, qk_w), lambda i: (i, 0)),
            pl.BlockSpec((tm, MLA_W), lambda i: (i, 0)),
        ],
        out_shape=[
            jax.ShapeDtypeStruct((TOKENS, qk_w), jnp.bfloat16),
            jax.ShapeDtypeStruct((TOKENS, qk_w), jnp.bfloat16),
            jax.ShapeDtypeStruct((TOKENS, MLA_W), jnp.bfloat16),
        ],
        compiler_params=_cparams(("parallel",)),
        name="mla_prep",
    )(x, w_small, q_g, kv_g, w_uq, w_ukv, cos_p, sin_p)


def _mla_attn_kernel(q_ref, k_ref, v_ref, g_ref, o_ref):
    row = lax.broadcasted_iota(jnp.int32, (ATT_TQ, ATT_TQ), 0)
    col = lax.broadcasted_iota(jnp.int32, (ATT_TQ, ATT_TQ), 1)
    v_aug = jnp.concatenate([v_ref[...], jnp.ones((SEQ, LANES), jnp.bfloat16)], axis=1)

    def scores_softmax(i):
        r0 = i * ATT_TQ
        n = r0 + ATT_TQ
        s = lax.dot_general(q_ref[r0:n, :], k_ref[0:n, :], (((1,), (1,)), ((), ())),
                            preferred_element_type=jnp.float32)
        diag = jnp.where(col <= row, s[:, r0:], NEG)
        s = diag if i == 0 else jnp.concatenate([s[:, :r0], diag], axis=1)
        m = jnp.max(s, axis=-1, keepdims=True)
        return jnp.exp2(s - m).astype(jnp.bfloat16)

    def weighted_values(i, p):
        r0 = i * ATT_TQ
        n = r0 + ATT_TQ
        acc = jnp.dot(p, v_aug[0:n, :], preferred_element_type=jnp.float32)
        gate = _silu(g_ref[r0:n, :].astype(jnp.float32))
        o_ref[r0:n, :] = (acc[:, :MLA_V] / acc[:, MLA_V:] * gate).astype(o_ref.dtype)

    pending = []
    for i in reversed(range(SEQ // ATT_TQ)):
        pending.append((i, scores_softmax(i)))
        if len(pending) > ATT_LOOKAHEAD:
            weighted_values(*pending.pop(0))
    for item in pending:
        weighted_values(*item)


def _mla_attention(q_full, k_full, v_m, h_main):
    qk_d = 2 * LANES
    return pl.pallas_call(
        _mla_attn_kernel,
        grid=(BATCH, MLA_HEADS),
        in_specs=[
            pl.BlockSpec((SEQ, qk_d), lambda b, h: (b, h)),
            pl.BlockSpec((SEQ, qk_d), lambda b, h: (b, h)),
            pl.BlockSpec((SEQ, MLA_V), lambda b, h: (b, h)),
            pl.BlockSpec((SEQ, MLA_V), lambda b, h: (b, MAIN_MG // MLA_V + h)),
        ],
        out_specs=pl.BlockSpec((SEQ, MLA_V), lambda b, h: (b, h)),
        out_shape=jax.ShapeDtypeStruct((TOKENS, MLA_W), jnp.bfloat16),
        compiler_params=_cparams(("parallel", "parallel")),
        name="mla_attention",
    )(q_full, k_full, v_m, h_main)


def _out_ln_kernel(*refs, feature_major):
    n_acts = len(feature_major)
    a_refs = refs[:n_acts]
    w_ref, x_ref, g_ref, b_ref, o_ref = refs[n_acts:]
    for r0 in range(0, o_ref.shape[0], OUT_SUB):
        y = None
        k0 = 0
        for a_ref, fm in zip(a_refs, feature_major):
            if fm:
                kw = a_ref.shape[0]
                t = lax.dot_general(a_ref[:, r0: r0 + OUT_SUB], w_ref[k0: k0 + kw, :], (((0,), (0,)), ((), ())),
                                    preferred_element_type=jnp.float32)
            else:
                kw = a_ref.shape[1]
                t = jnp.dot(a_ref[r0: r0 + OUT_SUB, :], w_ref[k0: k0 + kw, :], preferred_element_type=jnp.float32)
            y = t if y is None else y + t
            k0 += kw
        z = DN_ALPHA * x_ref[r0: r0 + OUT_SUB, :] + y
        mu = jnp.mean(z, axis=-1, keepdims=True)
        zc = z - mu
        var = jnp.mean(zc * zc, axis=-1, keepdims=True)
        o_ref[r0: r0 + OUT_SUB, :] = zc * lax.rsqrt(var + LN_EPS) * g_ref[...] + b_ref[...]


def _out_ln(acts, w, x, ln_g, ln_b, name, feature_major=None, tm=OUT_TM):
    feature_major = tuple(feature_major or (False,) * len(acts))
    in_specs = [pl.BlockSpec((a.shape[0], tm), lambda i: (0, i)) if fm else
                pl.BlockSpec((tm, a.shape[1]), lambda i: (i, 0)) for a, fm in zip(acts, feature_major)]
    in_specs += [_resident(w.shape), pl.BlockSpec((tm, D_MODEL), lambda i: (i, 0)),
                 _resident(ln_g.shape), _resident(ln_b.shape)]
    return pl.pallas_call(
        functools.partial(_out_ln_kernel, feature_major=feature_major),
        grid=(TOKENS // tm,),
        in_specs=in_specs,
        out_specs=pl.BlockSpec((tm, D_MODEL), lambda i: (i, 0)),
        out_shape=jax.ShapeDtypeStruct((TOKENS, D_MODEL), jnp.float32),
        compiler_params=_cparams(("parallel",)),
        name=name,
    )(*acts, w, x, ln_g, ln_b)


def _split3(x):
    hi = x.astype(jnp.bfloat16)
    r1 = x - hi.astype(jnp.float32)
    mid = r1.astype(jnp.bfloat16)
    lo = (r1 - mid.astype(jnp.float32)).astype(jnp.bfloat16)
    return hi, mid, lo


def _moba_kernel(q_ref, k_ref, vt_ref, gt_ref, o_ref):
    L = MOBA_BLOCK
    NB = MOBA_NB
    means = [jnp.mean(k_ref[n * L: (n + 1) * L, :].astype(jnp.float32), axis=0, keepdims=True) for n in range(NB)]
    km = jnp.concatenate(means, axis=0)
    parts = [p.astype(jnp.float32) for p in _split3(km)]
    km3 = jnp.concatenate(parts + [jnp.zeros_like(km)], axis=0).astype(jnp.bfloat16)

    q = q_ref[...]
    lane = lax.broadcasted_iota(jnp.int32, (SEQ, LANES), 1)
    second = (lane // (HALF // 2)) % 2
    blk = lax.broadcasted_iota(jnp.int32, (NB, SEQ), 0)
    qblk = lax.broadcasted_iota(jnp.int32, (NB, SEQ), 1) // L
    past = blk < qblk
    krow = lax.broadcasted_iota(jnp.int32, (L, L), 0)
    qcol = lax.broadcasted_iota(jnp.int32, (L, L), 1)
    hd = MOBA_DH
    qes, sels = [], []
    for e in range(2):
        qe = jnp.where(second == e, q, jnp.zeros_like(q))
        g3 = lax.dot_general(km3, qe, (((1,), (1,)), ((), ())), preferred_element_type=jnp.float32)
        gate = jnp.where(past, g3[0:NB] + g3[NB: 2 * NB] + g3[2 * NB: 3 * NB], NEG)
        rank = jnp.zeros((NB, SEQ), jnp.float32)
        for n in range(NB - 1):
            gn = gate[n: n + 1, :]
            rank = rank + jnp.where(gn > gate, 1.0, jnp.where(gn == gate, jnp.where(blk > n, 1.0, 0.0), 0.0))
        qes.append(qe)
        sels.append(jnp.where(past, jnp.where(rank < float(MOBA_TOPK), 1.0, 0.0), 0.0))

    def scores_softmax(e, i):
        r0 = i * L
        n_keys = r0 + L
        st = lax.dot_general(k_ref[0:n_keys, :], qes[e][r0:n_keys, :], (((1,), (1,)), ((), ())),
                             preferred_element_type=jnp.float32)
        pieces = [jnp.where(sels[e][j: j + 1, r0:n_keys] > 0.0, st[j * L: (j + 1) * L, :], NEG) for j in range(i)]
        pieces.append(jnp.where(krow <= qcol, st[r0:, :], NEG))
        sm = pieces[0] if i == 0 else jnp.concatenate(pieces, axis=0)
        m = jnp.max(sm, axis=0, keepdims=True)
        return (jnp.exp2(sm - m).astype(jnp.bfloat16),)

    ones = jnp.ones((BF16_SUBLANES, SEQ), jnp.bfloat16)
    vt_aug = [jnp.concatenate([vt_ref[e * hd: (e + 1) * hd, :], ones], axis=0) for e in range(2)]

    def weighted_values(e, i, p):
        r0 = i * L
        n_keys = r0 + L
        ot = jnp.dot(vt_aug[e][:, 0:n_keys], p, preferred_element_type=jnp.float32)
        gate_t = _silu(gt_ref[e * hd: (e + 1) * hd, r0:n_keys].astype(jnp.float32))
        o_ref[e * hd: (e + 1) * hd, r0:n_keys] = (ot[:hd] / ot[hd: hd + 1] * gate_t).astype(o_ref.dtype)

    units = [(e, i) for i in reversed(range(NB)) for e in range(2)]
    pending = []
    for unit in units:
        pending.append(unit + scores_softmax(*unit))
        if len(pending) > MOBA_LOOKAHEAD:
            weighted_values(*pending.pop(0))
    for item in pending:
        weighted_values(*item)


def _moba(h_qk, h_vg_t):
    npair = MOBA_HEADS // 2
    return pl.pallas_call(
        _moba_kernel,
        grid=(BATCH, npair),
        in_specs=[
            pl.BlockSpec((SEQ, LANES), lambda b, p: (b, p)),
            pl.BlockSpec((SEQ, LANES), lambda b, p: (b, npair + p)),
            pl.BlockSpec((LANES, SEQ), lambda b, p: (p, b)),
            pl.BlockSpec((LANES, SEQ), lambda b, p: (npair + p, b)),
        ],
        out_specs=pl.BlockSpec((LANES, SEQ), lambda b, p: (p, b)),
        out_shape=jax.ShapeDtypeStruct((MOBA_W, TOKENS), jnp.bfloat16),
        compiler_params=_cparams(("parallel", "parallel")),
        name="moba_attention",
    )(h_qk, h_qk, h_vg_t, h_vg_t)


def kernel(x, positions, w_in_even, q_norm_even, w_uq_even, kv_norm_even, w_ukv_even, w_out_even, w_in_odd,
           w_out_odd, ln_g, ln_b):
    bf = jnp.bfloat16
    x0 = x.reshape(TOKENS, D_MODEL)
    pos = positions.reshape(TOKENS, 1)

    inv_r = (1.0 / (ROPE_THETA ** jnp.linspace(0.0, 1.0, RET_DK // 2, dtype=jnp.float32))).reshape(1, LANES)
    inv_rope = 1.0 / (ROPE_THETA ** (jnp.arange(0, MOBA_DH, 2, dtype=jnp.float32) / MOBA_DH))
    inv_p = jnp.tile(inv_rope, LANES // (MOBA_DH // 2)).reshape(1, LANES)
    cos_r, sin_r, cos_p, sin_p = _tables(pos, inv_r, inv_p)

    def permute_groups(w, src):
        k, n = w.shape
        width = len(src)
        sel = np.zeros((width, width), np.float32)
        sel[src, np.arange(width)] = 1.0
        return jnp.einsum("kgc,cd->kgd", w.reshape(k, n // width, width), sel).reshape(k, n)

    quarter = HALF // 2
    pair_src = np.concatenate([np.arange(0, quarter), np.arange(HALF, HALF + quarter),
                               np.arange(quarter, HALF), np.arange(HALF + quarter, LANES)])
    pair_layout = lambda w: permute_groups(w, pair_src)

    w_in = w_in_even[0]
    o = np.cumsum((0, 1024, 1024, RET_W, RET_W, MLA_Q_RANK, MLA_KV_RANK, MLA_ROPE, MLA_W))
    w_rqk = permute_groups(w_in[:, :o[2]],
                           np.concatenate([np.arange(0, RET_DK, 2), np.arange(1, RET_DK, 2)])).astype(bf)
    w_in_bf = w_in.astype(bf)
    w_mg = w_in_bf[:, o[7]: o[8]]
    mkr = w_in_bf[:, o[6]: o[7]].reshape(D_MODEL, 2, 1, MLA_ROPE // 2)
    kr_dup = jnp.broadcast_to(mkr, (D_MODEL, 2, 2, MLA_ROPE // 2)).reshape(D_MODEL, LANES)
    w_small = jnp.concatenate([w_in_bf[:, o[4]: o[6]], kr_dup], axis=1)
    w_uq = w_uq_even[0].reshape(MLA_Q_RANK, MLA_HEADS, MLA_NOPE + MLA_ROPE)
    w_uq_p = jnp.concatenate([w_uq[:, :, :MLA_NOPE].reshape(MLA_Q_RANK, -1),
                              pair_layout(w_uq[:, :, MLA_NOPE:].reshape(MLA_Q_RANK, -1))], axis=1).astype(bf)
    w_ukv = w_ukv_even[0].reshape(MLA_KV_RANK, MLA_HEADS, MLA_NOPE + MLA_V)
    w_ukv_p = jnp.concatenate([w_ukv[:, :, :MLA_NOPE].reshape(MLA_KV_RANK, -1),
                               w_ukv[:, :, MLA_NOPE:].reshape(MLA_KV_RANK, -1)], axis=1).astype(bf)
    w_io = w_in_odd[0]
    w_qk = jnp.concatenate([pair_layout(w_io[:, :MOBA_W]), pair_layout(w_io[:, MOBA_W: 2 * MOBA_W])],
                           axis=1).astype(bf)
    w_vg_t = w_io[:, 2 * MOBA_W:].T.astype(bf)

    h_main = _project_even(x0, w_rqk, w_in_bf, w_mg, cos_r, sin_r)
    ret_out = _retention(h_main)
    q_full, k_full, v_m = _mla_prep(x0, w_small, q_norm_even[0].reshape(1, -1), kv_norm_even[0].reshape(1, -1),
                                    w_uq_p, w_ukv_p, cos_p, sin_p)
    mla_out = _mla_attention(q_full, k_full, v_m, h_main)
    x1 = _out_ln([ret_out, mla_out], w_out_even[0].astype(bf), x0,
                 ln_g[0].reshape(1, -1), ln_b[0].reshape(1, -1), name="out_ln_even")

    h_qk, h_vg_t = _project_odd(x1, w_qk, w_vg_t, cos_p, sin_p)
    moba_out_t = _moba(h_qk, h_vg_t)
    x2 = _out_ln([moba_out_t], w_out_odd[0].astype(bf), x1, ln_g[1].reshape(1, -1), ln_b[1].reshape(1, -1),
                 name="out_ln_odd", feature_major=(True,), tm=2 * OUT_TM)
    return x2.reshape(BATCH, SEQ, D_MODEL)
```

```python
import functools
import math

import numpy as np
import jax
import jax.numpy as jnp
from jax import lax
from jax.experimental import pallas as pl
from jax.experimental.pallas import tpu as pltpu

D_MODEL = 1024
BATCH = 8
SEQ = 2048
DEPTH = 2
TOKENS = BATCH * SEQ

RET_HEADS = 4
RET_DK = 256
RET_DV = 512
MLA_HEADS = 8
MLA_NOPE = 128
MLA_ROPE = 64
MLA_V = 128
MLA_Q_RANK = 256
MLA_KV_RANK = 256
MOBA_HEADS = 16
MOBA_DH = 64
MOBA_BLOCK = 256
MOBA_TOPK = 3
MOBA_NB = SEQ // MOBA_BLOCK
MOBA_LOOKAHEAD = 6
ATT_LOOKAHEAD = 3

ROPE_THETA = 10000.0
LN_EPS = 1e-5
RMS_EPS = 1e-6
NEG = -1e30
DN_ALPHA = (2.0 * DEPTH) ** 0.25
LOG2E = math.log2(math.e)

RET_W = RET_HEADS * RET_DV
MLA_W = MLA_HEADS * MLA_V
MOBA_W = MOBA_HEADS * MOBA_DH

LANES = 128
HALF = LANES // 2
BF16_SUBLANES = 16
VMEM_LIMIT = 48 * 1024 * 1024

MAIN_RQ, MAIN_RK, MAIN_RV, MAIN_RG, MAIN_MG = 0, 1024, 2048, 4096, 6144
MAIN_W = 7168
SMALL_W = MLA_Q_RANK + MLA_KV_RANK + LANES

PROJ_TM = 512
PROJ_TN = 1024
OUT_SUB = 256
RET_CHUNK = 256
RET_CHUNKS_PER_STEP = 8
ATT_TQ = 256
ATT_TK = 256
PREP_TM = 512
OUT_TM = 512
TAB_TM = 1024
ROT_TOKENS_PER_ROW = LANES // (MOBA_DH // 2)


def _cparams(sem):
    return pltpu.CompilerParams(dimension_semantics=sem, vmem_limit_bytes=VMEM_LIMIT)


def _silu(g):
    h = 0.5 * g
    return h + h * jnp.tanh(h)


def _tables_kernel(pos_ref, pos4_ref, inv_r_ref, inv_p_ref, cr_ref, sr_ref, cp_ref, sp_ref):
    pos = pos_ref[...].astype(jnp.float32)
    ang_r = pos * inv_r_ref[...]
    cr_ref[...] = jnp.cos(ang_r)
    sr_ref[...] = jnp.sin(ang_r)

    ang_p = pos4_ref[...].astype(jnp.float32) * inv_p_ref[...]
    cos_p = jnp.cos(ang_p)
    sin_p = jnp.sin(ang_p)
    lane = lax.broadcasted_iota(jnp.int32, ang_p.shape, 1)
    group = LANES // ROT_TOKENS_PER_ROW
    rows = ang_p.shape[0]

    def spread(x, a):
        x = jnp.where(lane // group == a, x, 0.0)
        out = x
        for t in range(1, ROT_TOKENS_PER_ROW):
            out = out + pltpu.roll(x, t * group, axis=1)
        return out

    for a in range(ROT_TOKENS_PER_ROW):
        cp_ref[pl.ds(a, rows, stride=ROT_TOKENS_PER_ROW), :] = spread(cos_p, a)
        s = spread(sin_p, a)
        sp_ref[pl.ds(a, rows, stride=ROT_TOKENS_PER_ROW), :] = jnp.where(lane < HALF, -s, s)


def _tables(pos, pos4, inv_r, inv_p):
    tab = jax.ShapeDtypeStruct((TOKENS, LANES), jnp.float32)
    row = pl.BlockSpec((TAB_TM, LANES), lambda i: (i, 0))
    cst = pl.BlockSpec((1, LANES), lambda i: (0, 0))
    return pl.pallas_call(
        _tables_kernel,
        grid=(TOKENS // TAB_TM,),
        in_specs=[pl.BlockSpec((TAB_TM, 1), lambda i: (i, 0)),
                  pl.BlockSpec((TAB_TM // ROT_TOKENS_PER_ROW, LANES), lambda i: (i, 0)), cst, cst],
        out_specs=[row, row, row, row],
        out_shape=[tab, tab, tab, tab],
        compiler_params=_cparams(("parallel",)),
        name="rot_tables",
    )(pos, pos4, inv_r, inv_p)


def _resident(shape):
    return pl.BlockSpec(shape, lambda i: (0,) * len(shape), pipeline_mode=pl.Buffered(1))


def _proj_even_kernel(x_ref, wqk_ref, wv_ref, wg_ref, wmg_ref, cos_ref, sin_ref, o_ref):
    xb = x_ref[...].astype(jnp.bfloat16)
    cos = cos_ref[...]
    sin = sin_ref[...]
    w_refs = {MAIN_RQ: wqk_ref, MAIN_RV: wv_ref, MAIN_RG: wg_ref, MAIN_MG: wmg_ref}
    for c0 in range(0, MAIN_W, PROJ_TN):
        base = max(b for b in w_refs if b <= c0)
        acc = jnp.dot(xb, w_refs[base][:, c0 - base: c0 - base + PROJ_TN], preferred_element_type=jnp.float32)
        if c0 not in (MAIN_RQ, MAIN_RK):
            o_ref[:, c0: c0 + PROJ_TN] = acc.astype(o_ref.dtype)
            continue
        for h0 in range(0, PROJ_TN, RET_DK):
            x1 = acc[:, h0: h0 + LANES]
            x2 = acc[:, h0 + LANES: h0 + RET_DK]
            r1 = x1 * cos - x2 * sin
            r2 = x2 * cos + x1 * sin
            if c0 == MAIN_RK:
                r1 = r1 * RET_DK ** -0.5
                r2 = r2 * RET_DK ** -0.5
            o_ref[:, c0 + h0: c0 + h0 + LANES] = r1.astype(o_ref.dtype)
            o_ref[:, c0 + h0 + LANES: c0 + h0 + RET_DK] = r2.astype(o_ref.dtype)


def _project_even(x, w_qk, w_in_bf, w_mg, cos, sin):
    tm = PROJ_TM
    row = lambda width: pl.BlockSpec((tm, width), lambda i: (i, 0))
    col_block = lambda j: pl.BlockSpec((D_MODEL, RET_W), lambda i: (0, j), pipeline_mode=pl.Buffered(1))
    return pl.pallas_call(
        _proj_even_kernel,
        grid=(TOKENS // tm,),
        in_specs=[row(D_MODEL), _resident(w_qk.shape), col_block(MAIN_RV // RET_W), col_block(MAIN_RG // RET_W),
                  _resident(w_mg.shape), row(LANES), row(LANES)],
        out_specs=row(MAIN_W),
        out_shape=jax.ShapeDtypeStruct((TOKENS, MAIN_W), jnp.bfloat16),
        compiler_params=_cparams(("parallel",)),
        name="proj_even",
    )(x, w_qk, w_in_bf, w_in_bf, w_mg, cos, sin)


def _proj_odd_kernel(x_ref, wqk_ref, wvgt_ref, cos_ref, sin_ref, oqk_ref, ovgt_ref):
    xb = x_ref[...].astype(jnp.bfloat16)
    cos = cos_ref[...]
    sin = sin_ref[...]
    for c0 in range(0, 2 * MOBA_W, PROJ_TN):
        acc = jnp.dot(xb, wqk_ref[:, c0: c0 + PROJ_TN], preferred_element_type=jnp.float32)
        for g0 in range(0, PROJ_TN, LANES):
            xg = acc[:, g0: g0 + LANES]
            rot = xg * cos + pltpu.roll(xg, HALF, axis=1) * sin
            if c0 < MOBA_W:
                rot = rot * (MOBA_DH ** -0.5 * LOG2E)
            oqk_ref[:, c0 + g0: c0 + g0 + LANES] = rot.astype(oqk_ref.dtype)
    for r0 in range(0, 2 * MOBA_W, PROJ_TN):
        ovgt_ref[r0: r0 + PROJ_TN, :] = lax.dot_general(
            wvgt_ref[r0: r0 + PROJ_TN, :], xb, (((1,), (1,)), ((), ())),
            preferred_element_type=jnp.float32).astype(ovgt_ref.dtype)


def _project_odd(x, w_qk, w_vg_t, cos, sin):
    tm = 2 * PROJ_TM
    row = lambda width: pl.BlockSpec((tm, width), lambda i: (i, 0))
    return pl.pallas_call(
        _proj_odd_kernel,
        grid=(TOKENS // tm,),
        in_specs=[row(D_MODEL), _resident(w_qk.shape), _resident(w_vg_t.shape), row(LANES), row(LANES)],
        out_specs=[row(2 * MOBA_W), pl.BlockSpec((2 * MOBA_W, tm), lambda i: (0, i))],
        out_shape=[jax.ShapeDtypeStruct((TOKENS, 2 * MOBA_W), jnp.bfloat16),
                   jax.ShapeDtypeStruct((2 * MOBA_W, TOKENS), jnp.bfloat16)],
        compiler_params=_cparams(("parallel",)),
        name="proj_odd",
    )(x, w_qk, w_vg_t, cos, sin)


def _retention_kernel(q_ref, k_ref, v_ref, g_ref, o_ref, state_ref, decay_ref, xi_ref, zeta_ref, gch_ref):
    C = RET_CHUNK

    @pl.when((pl.program_id(1) == 0) & (pl.program_id(2) == 0))
    def _():
        def log_gamma(shape):
            hf = jnp.full(shape, pl.program_id(0), jnp.int32).astype(jnp.float32)
            return jnp.log(1.0 - jnp.exp2(-5.0 - hf))

        row = lax.broadcasted_iota(jnp.int32, (C, C), 0)
        col = lax.broadcasted_iota(jnp.int32, (C, C), 1)
        rel = (row - col).astype(jnp.float32)
        decay_ref[...] = jnp.where(rel >= 0, jnp.exp(log_gamma((C, C)) * jnp.maximum(rel, 0.0)), 0.0)
        idx = lax.broadcasted_iota(jnp.int32, (C, LANES), 0).astype(jnp.float32)
        xi_ref[...] = jnp.exp(log_gamma((C, LANES)) * (idx + 1.0))
        zeta_ref[...] = jnp.exp(log_gamma((C, LANES)) * (C - 1.0 - idx))
        gch_ref[...] = jnp.exp(log_gamma(gch_ref.shape) * float(C))

    @pl.when(pl.program_id(2) == 0)
    def _():
        state_ref[...] = jnp.zeros_like(state_ref)

    decay = decay_ref[...]
    xi = jnp.concatenate([xi_ref[...]] * (RET_DV // LANES), axis=1)
    zeta = jnp.concatenate([zeta_ref[...]] * (RET_DK // LANES), axis=1)
    g_chunk = gch_ref[0:1, :]

    def finish(t, o):
        rows = slice(t * C, (t + 1) * C)
        mu = jnp.mean(o, axis=-1, keepdims=True)
        oc = o - mu
        var = jnp.mean(oc * oc, axis=-1, keepdims=True)
        on = oc * lax.rsqrt(var + LN_EPS)
        o_ref[rows, :] = (on * _silu(g_ref[rows, :].astype(jnp.float32))).astype(o_ref.dtype)

    prev = None
    for t in range(RET_CHUNKS_PER_STEP):
        rows = slice(t * C, (t + 1) * C)
        q = q_ref[rows, :]
        k = k_ref[rows, :]
        v = v_ref[rows, :]
        s = lax.dot_general(q, k, (((1,), (1,)), ((), ())), preferred_element_type=jnp.float32)
        state = state_ref[...]
        cross = jnp.dot(q, state.astype(jnp.bfloat16), preferred_element_type=jnp.float32)
        kz = (k.astype(jnp.float32) * zeta).astype(jnp.bfloat16)
        kv = lax.dot_general(kz, v, (((0,), (0,)), ((), ())), preferred_element_type=jnp.float32)
        state_ref[...] = state * g_chunk + kv
        inner = (s * decay).astype(jnp.bfloat16)
        o = jnp.dot(inner, v, preferred_element_type=jnp.float32) + cross * xi
        if prev is not None:
            finish(*prev)
        prev = (t, o)
    finish(*prev)


def _retention(h_main):
    C = RET_CHUNK
    rows = C * RET_CHUNKS_PER_STEP
    ns = SEQ // rows
    return pl.pallas_call(
        _retention_kernel,
        grid=(RET_HEADS, BATCH, ns),
        in_specs=[
            pl.BlockSpec((rows, RET_DK), lambda h, b, c: (b * ns + c, MAIN_RQ // RET_DK + h)),
            pl.BlockSpec((rows, RET_DK), lambda h, b, c: (b * ns + c, MAIN_RK // RET_DK + h)),
            pl.BlockSpec((rows, RET_DV), lambda h, b, c: (b * ns + c, MAIN_RV // RET_DV + h)),
            pl.BlockSpec((rows, RET_DV), lambda h, b, c: (b * ns + c, MAIN_RG // RET_DV + h)),
        ],
        out_specs=pl.BlockSpec((rows, RET_DV), lambda h, b, c: (b * ns + c, h)),
        out_shape=jax.ShapeDtypeStruct((TOKENS, RET_W), jnp.bfloat16),
        scratch_shapes=[
            pltpu.VMEM((RET_DK, RET_DV), jnp.float32),
            pltpu.VMEM((C, C), jnp.float32),
            pltpu.VMEM((C, LANES), jnp.float32),
            pltpu.VMEM((C, LANES), jnp.float32),
            pltpu.VMEM((8, RET_DV), jnp.float32),
        ],
        compiler_params=_cparams(("arbitrary", "arbitrary", "arbitrary")),
        name="retention",
    )(h_main, h_main, h_main, h_main)


def _rms_norm(x, g):
    return x * lax.rsqrt(jnp.mean(x * x, axis=-1, keepdims=True) + RMS_EPS) * g


def _mla_prep_kernel(x_ref, ws_ref, qg_ref, kvg_ref, wuq_ref, wukv_ref, cos_ref, sin_ref,
                     q_ref, k_ref, v_ref):
    xb = x_ref[...].astype(jnp.bfloat16)
    hs = jnp.dot(xb, ws_ref[...], preferred_element_type=jnp.float32)
    mq = hs[:, :MLA_Q_RANK]
    mkv = hs[:, MLA_Q_RANK: MLA_Q_RANK + MLA_KV_RANK]
    mkr = hs[:, MLA_Q_RANK + MLA_KV_RANK:]
    cos = cos_ref[...]
    sin = sin_ref[...]

    qn = _rms_norm(mq, qg_ref[...]).astype(jnp.bfloat16)
    kvn = _rms_norm(mkv, kvg_ref[...]).astype(jnp.bfloat16)
    qm = jnp.dot(qn, wuq_ref[...], preferred_element_type=jnp.float32)
    kv = jnp.dot(kvn, wukv_ref[...], preferred_element_type=jnp.float32)

    kpe = (mkr * cos + pltpu.roll(mkr, HALF, axis=1) * sin).astype(k_ref.dtype)
    scale = (MLA_NOPE + MLA_ROPE) ** -0.5 * LOG2E
    lane = lax.broadcasted_iota(jnp.int32, (x_ref.shape[0], LANES), 1)
    second = (lane // (HALF // 2)) % 2
    nope_w = MLA_HEADS * MLA_NOPE
    for p in range(MLA_HEADS // 2):
        pe = qm[:, nope_w + p * LANES: nope_w + (p + 1) * LANES]
        pe = (pe * cos + pltpu.roll(pe, HALF, axis=1) * sin) * scale
        for e in range(2):
            h = 2 * p + e
            base = h * 2 * LANES
            q_ref[:, base: base + LANES] = (qm[:, h * LANES: (h + 1) * LANES] * scale).astype(q_ref.dtype)
            q_ref[:, base + LANES: base + 2 * LANES] = jnp.where(second == e, pe, 0.0).astype(q_ref.dtype)
            k_ref[:, base: base + LANES] = kv[:, h * LANES: (h + 1) * LANES].astype(k_ref.dtype)
            k_ref[:, base + LANES: base + 2 * LANES] = kpe
    v_ref[...] = kv[:, nope_w:].astype(v_ref.dtype)


def _mla_prep(x, w_small, q_g, kv_g, w_uq, w_ukv, cos_p, sin_p):
    tm = PREP_TM
    full = lambda a: _resident(a.shape)
    qk_w = MLA_HEADS * 2 * LANES
    return pl.pallas_call(
        _mla_prep_kernel,
        grid=(TOKENS // tm,),
        in_specs=[
            pl.BlockSpec((tm, D_MODEL), lambda i: (i, 0)),
            full(w_small), full(q_g), full(kv_g), full(w_uq), full(w_ukv),
            pl.BlockSpec((tm, LANES), lambda i: (i, 0)),
            pl.BlockSpec((tm, LANES), lambda i: (i, 0)),
        ],
        out_specs=[
            pl.BlockSpec((tm, qk_w), lambda i: (i, 0)),
            pl.BlockSpec((tm, qk_w), lambda i: (i, 0)),
            pl.BlockSpec((tm, MLA_W), lambda i: (i, 0)),
        ],
        out_shape=[
            jax.ShapeDtypeStruct((TOKENS, qk_w), jnp.bfloat16),
            jax.ShapeDtypeStruct((TOKENS, qk_w), jnp.bfloat16),
            jax.ShapeDtypeStruct((TOKENS, MLA_W), jnp.bfloat16),
        ],
        compiler_params=_cparams(("parallel",)),
        name="mla_prep",
    )(x, w_small, q_g, kv_g, w_uq, w_ukv, cos_p, sin_p)


def _mla_attn_kernel(q_ref, k_ref, v_ref, g_ref, o_ref):
    row = lax.broadcasted_iota(jnp.int32, (ATT_TQ, ATT_TQ), 0)
    col = lax.broadcasted_iota(jnp.int32, (ATT_TQ, ATT_TQ), 1)
    v_aug = jnp.concatenate([v_ref[...], jnp.ones((SEQ, LANES), jnp.bfloat16)], axis=1)

    def scores_softmax(i):
        r0 = i * ATT_TQ
        n = r0 + ATT_TQ
        s = lax.dot_general(q_ref[r0:n, :], k_ref[0:n, :], (((1,), (1,)), ((), ())),
                            preferred_element_type=jnp.float32)
        diag = jnp.where(col <= row, s[:, r0:], NEG)
        s = diag if i == 0 else jnp.concatenate([s[:, :r0], diag], axis=1)
        m = jnp.max(s, axis=-1, keepdims=True)
        return jnp.exp2(s - m).astype(jnp.bfloat16)

    def weighted_values(i, p):
        r0 = i * ATT_TQ
        n = r0 + ATT_TQ
        acc = jnp.dot(p, v_aug[0:n, :], preferred_element_type=jnp.float32)
        gate = _silu(g_ref[r0:n, :].astype(jnp.float32))
        o_ref[r0:n, :] = (acc[:, :MLA_V] / acc[:, MLA_V:] * gate).astype(o_ref.dtype)

    pending = []
    for i in reversed(range(SEQ // ATT_TQ)):
        pending.append((i, scores_softmax(i)))
        if len(pending) > ATT_LOOKAHEAD:
            weighted_values(*pending.pop(0))
    for item in pending:
        weighted_values(*item)


def _mla_attention(q_full, k_full, v_m, h_main):
    qk_d = 2 * LANES
    return pl.pallas_call(
        _mla_attn_kernel,
        grid=(BATCH, MLA_HEADS),
        in_specs=[
            pl.BlockSpec((SEQ, qk_d), lambda b, h: (b, h)),
            pl.BlockSpec((SEQ, qk_d), lambda b, h: (b, h)),
            pl.BlockSpec((SEQ, MLA_V), lambda b, h: (b, h)),
            pl.BlockSpec((SEQ, MLA_V), lambda b, h: (b, MAIN_MG // MLA_V + h)),
        ],
        out_specs=pl.BlockSpec((SEQ, MLA_V), lambda b, h: (b, h)),
        out_shape=jax.ShapeDtypeStruct((TOKENS, MLA_W), jnp.bfloat16),
        compiler_params=_cparams(("parallel", "parallel")),
        name="mla_attention",
    )(q_full, k_full, v_m, h_main)


def _out_ln_kernel(*refs, feature_major):
    n_acts = len(feature_major)
    a_refs = refs[:n_acts]
    w_ref, x_ref, g_ref, b_ref, o_ref = refs[n_acts:]
    for r0 in range(0, o_ref.shape[0], OUT_SUB):
        y = None
        k0 = 0
        for a_ref, fm in zip(a_refs, feature_major):
            if fm:
                kw = a_ref.shape[0]
                t = lax.dot_general(a_ref[:, r0: r0 + OUT_SUB], w_ref[k0: k0 + kw, :], (((0,), (0,)), ((), ())),
                                    preferred_element_type=jnp.float32)
            else:
                kw = a_ref.shape[1]
                t = jnp.dot(a_ref[r0: r0 + OUT_SUB, :], w_ref[k0: k0 + kw, :], preferred_element_type=jnp.float32)
            y = t if y is None else y + t
            k0 += kw
        z = DN_ALPHA * x_ref[r0: r0 + OUT_SUB, :] + y
        mu = jnp.mean(z, axis=-1, keepdims=True)
        zc = z - mu
        var = jnp.mean(zc * zc, axis=-1, keepdims=True)
        o_ref[r0: r0 + OUT_SUB, :] = zc * lax.rsqrt(var + LN_EPS) * g_ref[...] + b_ref[...]


def _out_ln(acts, w, x, ln_g, ln_b, name, feature_major=None, tm=OUT_TM):
    feature_major = tuple(feature_major or (False,) * len(acts))
    in_specs = [pl.BlockSpec((a.shape[0], tm), lambda i: (0, i)) if fm else
                pl.BlockSpec((tm, a.shape[1]), lambda i: (i, 0)) for a, fm in zip(acts, feature_major)]
    in_specs += [_resident(w.shape), pl.BlockSpec((tm, D_MODEL), lambda i: (i, 0)),
                 _resident(ln_g.shape), _resident(ln_b.shape)]
    return pl.pallas_call(
        functools.partial(_out_ln_kernel, feature_major=feature_major),
        grid=(TOKENS // tm,),
        in_specs=in_specs,
        out_specs=pl.BlockSpec((tm, D_MODEL), lambda i: (i, 0)),
        out_shape=jax.ShapeDtypeStruct((TOKENS, D_MODEL), jnp.float32),
        compiler_params=_cparams(("parallel",)),
        name=name,
    )(*acts, w, x, ln_g, ln_b)


def _split3(x):
    hi = x.astype(jnp.bfloat16)
    r1 = x - hi.astype(jnp.float32)
    mid = r1.astype(jnp.bfloat16)
    lo = (r1 - mid.astype(jnp.float32)).astype(jnp.bfloat16)
    return hi, mid, lo


def _moba_kernel(q_ref, k_ref, vt_ref, gt_ref, o_ref):
    L = MOBA_BLOCK
    NB = MOBA_NB
    means = [jnp.mean(k_ref[n * L: (n + 1) * L, :].astype(jnp.float32), axis=0, keepdims=True) for n in range(NB)]
    km = jnp.concatenate(means, axis=0)
    parts = [p.astype(jnp.float32) for p in _split3(km)]
    km3 = jnp.concatenate(parts + [jnp.zeros_like(km)], axis=0)
    mlane = lax.broadcasted_iota(jnp.int32, km3.shape, 1)
    msecond = (mlane // (HALF // 2)) % 2
    km_pair = jnp.concatenate([jnp.where(msecond == e, km3, 0.0) for e in range(2)], axis=0).astype(jnp.bfloat16)

    q = q_ref[...]
    g_pair = lax.dot_general(km_pair, q, (((1,), (1,)), ((), ())), preferred_element_type=jnp.float32)
    lane = lax.broadcasted_iota(jnp.int32, (SEQ, LANES), 1)
    second = (lane // (HALF // 2)) % 2
    blk = lax.broadcasted_iota(jnp.int32, (NB, SEQ), 0)
    qblk = lax.broadcasted_iota(jnp.int32, (NB, SEQ), 1) // L
    past = blk < qblk
    krow = lax.broadcasted_iota(jnp.int32, (L, L), 0)
    qcol = lax.broadcasted_iota(jnp.int32, (L, L), 1)
    hd = MOBA_DH
    qes, sels = [], []
    for e in range(2):
        qe = jnp.where(second == e, q, jnp.zeros_like(q))
        g3 = g_pair[e * 4 * NB: (e + 1) * 4 * NB]
        gate = jnp.where(past, g3[0:NB] + g3[NB: 2 * NB] + g3[2 * NB: 3 * NB], NEG)
        rank = jnp.zeros((NB, SEQ), jnp.float32)
        for n in range(NB - 1):
            gn = gate[n: n + 1, :]
            rank = rank + jnp.where(gn > gate, 1.0, jnp.where(gn == gate, jnp.where(blk > n, 1.0, 0.0), 0.0))
        qes.append(qe)
        sels.append(jnp.where(past, jnp.where(rank < float(MOBA_TOPK), 1.0, 0.0), 0.0))

    ones = jnp.ones((BF16_SUBLANES, SEQ), jnp.bfloat16)
    vt_aug = [jnp.concatenate([vt_ref[e * hd: (e + 1) * hd, :], ones], axis=0) for e in range(2)]

    run_max, run_acc = {}, {}

    def block_scores(e, i, j):
        r0 = i * L
        st = lax.dot_general(k_ref[j * L: (j + 1) * L, :], qes[e][r0: r0 + L, :], (((1,), (1,)), ((), ())),
                             preferred_element_type=jnp.float32)
        keep = (krow <= qcol) if j == i else (sels[e][j: j + 1, r0: r0 + L] > 0.0)
        st = jnp.where(keep, st, NEG)
        bm = jnp.max(st, axis=0, keepdims=True)
        if (e, i) in run_max:
            m_new = jnp.maximum(run_max[(e, i)], bm)
            alpha = jnp.exp2(run_max[(e, i)] - m_new)
        else:
            m_new, alpha = bm, None
        run_max[(e, i)] = m_new
        return jnp.exp2(st - m_new).astype(jnp.bfloat16), alpha

    def block_values(e, i, j, last, p, alpha):
        r0 = i * L
        pv = jnp.dot(vt_aug[e][:, j * L: (j + 1) * L], p, preferred_element_type=jnp.float32)
        acc = pv if alpha is None else alpha * run_acc[(e, i)] + pv
        run_acc[(e, i)] = acc
        if last:
            gate_t = _silu(gt_ref[e * hd: (e + 1) * hd, r0: r0 + L].astype(jnp.float32))
            o_ref[e * hd: (e + 1) * hd, r0: r0 + L] = (acc[:hd] / acc[hd: hd + 1] * gate_t).astype(o_ref.dtype)

    chains = [(e, i) for i in reversed(range(NB)) for e in range(2)]
    steps = []
    for s in range(NB):
        for e, i in chains:
            if s <= i:
                steps.append((e, i, i if s == 0 else s - 1, s == i))
    pending = []
    for step in steps:
        pending.append(step + block_scores(*step[:3]))
        if len(pending) > MOBA_LOOKAHEAD:
            block_values(*pending.pop(0))
    for item in pending:
        block_values(*item)


def _moba(h_qk, h_vg_t):
    npair = MOBA_HEADS // 2
    return pl.pallas_call(
        _moba_kernel,
        grid=(BATCH, npair),
        in_specs=[
            pl.BlockSpec((SEQ, LANES), lambda b, p: (b, p)),
            pl.BlockSpec((SEQ, LANES), lambda b, p: (b, npair + p)),
            pl.BlockSpec((LANES, SEQ), lambda b, p: (p, b)),
            pl.BlockSpec((LANES, SEQ), lambda b, p: (npair + p, b)),
        ],
        out_specs=pl.BlockSpec((LANES, SEQ), lambda b, p: (p, b)),
        out_shape=jax.ShapeDtypeStruct((MOBA_W, TOKENS), jnp.bfloat16),
        compiler_params=_cparams(("parallel", "parallel")),
        name="moba_attention",
    )(h_qk, h_qk, h_vg_t, h_vg_t)


def kernel(x, positions, w_in_even, q_norm_even, w_uq_even, kv_norm_even, w_ukv_even, w_out_even, w_in_odd,
           w_out_odd, ln_g, ln_b):
    bf = jnp.bfloat16
    x0 = x.reshape(TOKENS, D_MODEL)
    pos = positions.reshape(TOKENS, 1)

    inv_r = (1.0 / (ROPE_THETA ** jnp.linspace(0.0, 1.0, RET_DK // 2, dtype=jnp.float32))).reshape(1, LANES)
    inv_rope = 1.0 / (ROPE_THETA ** (jnp.arange(0, MOBA_DH, 2, dtype=jnp.float32) / MOBA_DH))
    inv_p = jnp.tile(inv_rope, LANES // (MOBA_DH // 2)).reshape(1, LANES)
    pos4 = jnp.repeat(positions.reshape(TOKENS // ROT_TOKENS_PER_ROW, ROT_TOKENS_PER_ROW),
                      LANES // ROT_TOKENS_PER_ROW, axis=1)
    cos_r, sin_r, cos_p, sin_p = _tables(pos, pos4, inv_r, inv_p)

    def permute_groups(w, src):
        k, n = w.shape
        width = len(src)
        sel = np.zeros((width, width), np.float32)
        sel[src, np.arange(width)] = 1.0
        return jnp.einsum("kgc,cd->kgd", w.reshape(k, n // width, width), sel).reshape(k, n)

    quarter = HALF // 2
    pair_src = np.concatenate([np.arange(0, quarter), np.arange(HALF, HALF + quarter),
                               np.arange(quarter, HALF), np.arange(HALF + quarter, LANES)])
    pair_layout = lambda w: permute_groups(w, pair_src)

    w_in = w_in_even[0]
    o = np.cumsum((0, 1024, 1024, RET_W, RET_W, MLA_Q_RANK, MLA_KV_RANK, MLA_ROPE, MLA_W))
    w_rqk = permute_groups(w_in[:, :o[2]],
                           np.concatenate([np.arange(0, RET_DK, 2), np.arange(1, RET_DK, 2)])).astype(bf)
    w_in_bf = w_in.astype(bf)
    w_mg = w_in_bf[:, o[7]: o[8]]
    mkr = w_in_bf[:, o[6]: o[7]].reshape(D_MODEL, 2, 1, MLA_ROPE // 2)
    kr_dup = jnp.broadcast_to(mkr, (D_MODEL, 2, 2, MLA_ROPE // 2)).reshape(D_MODEL, LANES)
    w_small = jnp.concatenate([w_in_bf[:, o[4]: o[6]], kr_dup], axis=1)
    w_uq = w_uq_even[0].reshape(MLA_Q_RANK, MLA_HEADS, MLA_NOPE + MLA_ROPE)
    w_uq_p = jnp.concatenate([w_uq[:, :, :MLA_NOPE].reshape(MLA_Q_RANK, -1),
                              pair_layout(w_uq[:, :, MLA_NOPE:].reshape(MLA_Q_RANK, -1))], axis=1).astype(bf)
    w_ukv = w_ukv_even[0].reshape(MLA_KV_RANK, MLA_HEADS, MLA_NOPE + MLA_V)
    w_ukv_p = jnp.concatenate([w_ukv[:, :, :MLA_NOPE].reshape(MLA_KV_RANK, -1),
                               w_ukv[:, :, MLA_NOPE:].reshape(MLA_KV_RANK, -1)], axis=1).astype(bf)
    w_io = w_in_odd[0]
    w_qk = jnp.concatenate([pair_layout(w_io[:, :MOBA_W]), pair_layout(w_io[:, MOBA_W: 2 * MOBA_W])],
                           axis=1).astype(bf)
    w_vg_t = w_io[:, 2 * MOBA_W:].T.astype(bf)

    h_main = _project_even(x0, w_rqk, w_in_bf, w_mg, cos_r, sin_r)
    ret_out = _retention(h_main)
    q_full, k_full, v_m = _mla_prep(x0, w_small, q_norm_even[0].reshape(1, -1), kv_norm_even[0].reshape(1, -1),
                                    w_uq_p, w_ukv_p, cos_p, sin_p)
    mla_out = _mla_attention(q_full, k_full, v_m, h_main)
    x1 = _out_ln([ret_out, mla_out], w_out_even[0].astype(bf), x0,
                 ln_g[0].reshape(1, -1), ln_b[0].reshape(1, -1), name="out_ln_even", tm=2 * OUT_TM)

    h_qk, h_vg_t = _project_odd(x1, w_qk, w_vg_t, cos_p, sin_p)
    moba_out_t = _moba(h_qk, h_vg_t)
    x2 = _out_ln([moba_out_t], w_out_odd[0].astype(bf), x1, ln_g[1].reshape(1, -1), ln_b[1].reshape(1, -1),
                 name="out_ln_odd", feature_major=(True,), tm=2 * OUT_TM)
    return x2.reshape(BATCH, SEQ, D_MODEL)
```

```python
import functools
import math

import numpy as np
import jax
import jax.numpy as jnp
from jax import lax
from jax.experimental import pallas as pl
from jax.experimental.pallas import tpu as pltpu

D_MODEL = 1024
BATCH = 8
SEQ = 2048
DEPTH = 2
TOKENS = BATCH * SEQ

RET_HEADS = 4
RET_DK = 256
RET_DV = 512
MLA_HEADS = 8
MLA_NOPE = 128
MLA_ROPE = 64
MLA_V = 128
MLA_Q_RANK = 256
MLA_KV_RANK = 256
MOBA_HEADS = 16
MOBA_DH = 64
MOBA_BLOCK = 256
MOBA_TOPK = 3
MOBA_NB = SEQ // MOBA_BLOCK
MOBA_LOOKAHEAD = 6
ATT_LOOKAHEAD = 3

ROPE_THETA = 10000.0
LN_EPS = 1e-5
RMS_EPS = 1e-6
NEG = -1e30
DN_ALPHA = (2.0 * DEPTH) ** 0.25
LOG2E = math.log2(math.e)

RET_W = RET_HEADS * RET_DV
MLA_W = MLA_HEADS * MLA_V
MOBA_W = MOBA_HEADS * MOBA_DH

LANES = 128
HALF = LANES // 2
BF16_SUBLANES = 16
VMEM_LIMIT = 48 * 1024 * 1024

MAIN_RQ, MAIN_RK, MAIN_RV, MAIN_RG, MAIN_MG = 0, 1024, 2048, 4096, 6144
MAIN_W = 7168
SMALL_W = MLA_Q_RANK + MLA_KV_RANK + LANES

PROJ_TM = 512
PROJ_TN = 1024
OUT_SUB = 256
RET_CHUNK = 256
RET_CHUNKS_PER_STEP = 8
ATT_TQ = 256
ATT_TK = 256
PREP_TM = 512
OUT_TM = 512
TAB_TM = 1024
ROT_TOKENS_PER_ROW = LANES // (MOBA_DH // 2)


def _cparams(sem):
    return pltpu.CompilerParams(dimension_semantics=sem, vmem_limit_bytes=VMEM_LIMIT)


def _silu(g):
    h = 0.5 * g
    return h + h * jnp.tanh(h)


def _tables_kernel(pos_ref, pos4_ref, inv_r_ref, inv_p_ref, cr_ref, sr_ref, cp_ref, sp_ref):
    pos = pos_ref[...].astype(jnp.float32)
    ang_r = pos * inv_r_ref[...]
    cr_ref[...] = jnp.cos(ang_r)
    sr_ref[...] = jnp.sin(ang_r)

    ang_p = pos4_ref[...].astype(jnp.float32) * inv_p_ref[...]
    cos_p = jnp.cos(ang_p)
    sin_p = jnp.sin(ang_p)
    lane = lax.broadcasted_iota(jnp.int32, ang_p.shape, 1)
    group = LANES // ROT_TOKENS_PER_ROW
    rows = ang_p.shape[0]

    def spread(x, a):
        x = jnp.where(lane // group == a, x, 0.0)
        out = x
        for t in range(1, ROT_TOKENS_PER_ROW):
            out = out + pltpu.roll(x, t * group, axis=1)
        return out

    for a in range(ROT_TOKENS_PER_ROW):
        cp_ref[pl.ds(a, rows, stride=ROT_TOKENS_PER_ROW), :] = spread(cos_p, a)
        s = spread(sin_p, a)
        sp_ref[pl.ds(a, rows, stride=ROT_TOKENS_PER_ROW), :] = jnp.where(lane < HALF, -s, s)


def _tables(pos, pos4, inv_r, inv_p):
    tab = jax.ShapeDtypeStruct((TOKENS, LANES), jnp.float32)
    row = pl.BlockSpec((TAB_TM, LANES), lambda i: (i, 0))
    cst = pl.BlockSpec((1, LANES), lambda i: (0, 0))
    return pl.pallas_call(
        _tables_kernel,
        grid=(TOKENS // TAB_TM,),
        in_specs=[pl.BlockSpec((TAB_TM, 1), lambda i: (i, 0)),
                  pl.BlockSpec((TAB_TM // ROT_TOKENS_PER_ROW, LANES), lambda i: (i, 0)), cst, cst],
        out_specs=[row, row, row, row],
        out_shape=[tab, tab, tab, tab],
        compiler_params=_cparams(("parallel",)),
        name="rot_tables",
    )(pos, pos4, inv_r, inv_p)


def _resident(shape):
    return pl.BlockSpec(shape, lambda i: (0,) * len(shape), pipeline_mode=pl.Buffered(1))


def _proj_even_kernel(x_ref, wqk_ref, wv_ref, wg_ref, wmg_ref, cos_ref, sin_ref, o_ref):
    xb = x_ref[...].astype(jnp.bfloat16)
    cos = cos_ref[...]
    sin = sin_ref[...]
    w_refs = {MAIN_RQ: wqk_ref, MAIN_RV: wv_ref, MAIN_RG: wg_ref, MAIN_MG: wmg_ref}
    for c0 in range(0, MAIN_W, PROJ_TN):
        base = max(b for b in w_refs if b <= c0)
        acc = jnp.dot(xb, w_refs[base][:, c0 - base: c0 - base + PROJ_TN], preferred_element_type=jnp.float32)
        if c0 not in (MAIN_RQ, MAIN_RK):
            o_ref[:, c0: c0 + PROJ_TN] = acc.astype(o_ref.dtype)
            continue
        for h0 in range(0, PROJ_TN, RET_DK):
            x1 = acc[:, h0: h0 + LANES]
            x2 = acc[:, h0 + LANES: h0 + RET_DK]
            r1 = x1 * cos - x2 * sin
            r2 = x2 * cos + x1 * sin
            if c0 == MAIN_RK:
                r1 = r1 * RET_DK ** -0.5
                r2 = r2 * RET_DK ** -0.5
            o_ref[:, c0 + h0: c0 + h0 + LANES] = r1.astype(o_ref.dtype)
            o_ref[:, c0 + h0 + LANES: c0 + h0 + RET_DK] = r2.astype(o_ref.dtype)


def _project_even(x, w_qk, w_in_bf, w_mg, cos, sin):
    tm = PROJ_TM
    row = lambda width: pl.BlockSpec((tm, width), lambda i: (i, 0))
    col_block = lambda j: pl.BlockSpec((D_MODEL, RET_W), lambda i: (0, j), pipeline_mode=pl.Buffered(1))
    return pl.pallas_call(
        _proj_even_kernel,
        grid=(TOKENS // tm,),
        in_specs=[row(D_MODEL), _resident(w_qk.shape), col_block(MAIN_RV // RET_W), col_block(MAIN_RG // RET_W),
                  _resident(w_mg.shape), row(LANES), row(LANES)],
        out_specs=row(MAIN_W),
        out_shape=jax.ShapeDtypeStruct((TOKENS, MAIN_W), jnp.bfloat16),
        compiler_params=_cparams(("parallel",)),
        name="proj_even",
    )(x, w_qk, w_in_bf, w_in_bf, w_mg, cos, sin)


def _proj_odd_kernel(x_ref, wqk_ref, wvgt_ref, cos_ref, sin_ref, oqk_ref, ovgt_ref):
    xb = x_ref[...].astype(jnp.bfloat16)
    cos = cos_ref[...]
    sin = sin_ref[...]
    for c0 in range(0, 2 * MOBA_W, PROJ_TN):
        acc = jnp.dot(xb, wqk_ref[:, c0: c0 + PROJ_TN], preferred_element_type=jnp.float32)
        for g0 in range(0, PROJ_TN, LANES):
            xg = acc[:, g0: g0 + LANES]
            rot = xg * cos + pltpu.roll(xg, HALF, axis=1) * sin
            if c0 < MOBA_W:
                rot = rot * (MOBA_DH ** -0.5 * LOG2E)
            oqk_ref[:, c0 + g0: c0 + g0 + LANES] = rot.astype(oqk_ref.dtype)
    for r0 in range(0, 2 * MOBA_W, PROJ_TN):
        ovgt_ref[r0: r0 + PROJ_TN, :] = lax.dot_general(
            wvgt_ref[r0: r0 + PROJ_TN, :], xb, (((1,), (1,)), ((), ())),
            preferred_element_type=jnp.float32).astype(ovgt_ref.dtype)


def _project_odd(x, w_qk, w_vg_t, cos, sin):
    tm = 2 * PROJ_TM
    row = lambda width: pl.BlockSpec((tm, width), lambda i: (i, 0))
    return pl.pallas_call(
        _proj_odd_kernel,
        grid=(TOKENS // tm,),
        in_specs=[row(D_MODEL), _resident(w_qk.shape), _resident(w_vg_t.shape), row(LANES), row(LANES)],
        out_specs=[row(2 * MOBA_W), pl.BlockSpec((2 * MOBA_W, tm), lambda i: (0, i))],
        out_shape=[jax.ShapeDtypeStruct((TOKENS, 2 * MOBA_W), jnp.bfloat16),
                   jax.ShapeDtypeStruct((2 * MOBA_W, TOKENS), jnp.bfloat16)],
        compiler_params=_cparams(("parallel",)),
        name="proj_odd",
    )(x, w_qk, w_vg_t, cos, sin)


def _retention_kernel(q_ref, k_ref, v_ref, g_ref, o_ref, state_ref, decay_ref, xi_ref, zeta_ref, gch_ref):
    C = RET_CHUNK

    @pl.when((pl.program_id(1) == 0) & (pl.program_id(2) == 0))
    def _():
        def log_gamma(shape):
            hf = jnp.full(shape, pl.program_id(0), jnp.int32).astype(jnp.float32)
            return jnp.log(1.0 - jnp.exp2(-5.0 - hf))

        row = lax.broadcasted_iota(jnp.int32, (C, C), 0)
        col = lax.broadcasted_iota(jnp.int32, (C, C), 1)
        rel = (row - col).astype(jnp.float32)
        decay_ref[...] = jnp.where(rel >= 0, jnp.exp(log_gamma((C, C)) * jnp.maximum(rel, 0.0)), 0.0)
        idx = lax.broadcasted_iota(jnp.int32, (C, LANES), 0).astype(jnp.float32)
        xi_ref[...] = jnp.exp(log_gamma((C, LANES)) * (idx + 1.0))
        zeta_ref[...] = jnp.exp(log_gamma((C, LANES)) * (C - 1.0 - idx))
        gch_ref[...] = jnp.exp(log_gamma(gch_ref.shape) * float(C))

    @pl.when(pl.program_id(2) == 0)
    def _():
        state_ref[...] = jnp.zeros_like(state_ref)

    decay = decay_ref[...]
    xi = jnp.concatenate([xi_ref[...]] * (RET_DV // LANES), axis=1)
    zeta = jnp.concatenate([zeta_ref[...]] * (RET_DK // LANES), axis=1)
    g_chunk = gch_ref[0:1, :]

    def finish(t, o):
        rows = slice(t * C, (t + 1) * C)
        mu = jnp.mean(o, axis=-1, keepdims=True)
        oc = o - mu
        var = jnp.mean(oc * oc, axis=-1, keepdims=True)
        on = oc * lax.rsqrt(var + LN_EPS)
        o_ref[rows, :] = (on * _silu(g_ref[rows, :].astype(jnp.float32))).astype(o_ref.dtype)

    prev = None
    for t in range(RET_CHUNKS_PER_STEP):
        rows = slice(t * C, (t + 1) * C)
        q = q_ref[rows, :]
        k = k_ref[rows, :]
        v = v_ref[rows, :]
        s = lax.dot_general(q, k, (((1,), (1,)), ((), ())), preferred_element_type=jnp.float32)
        state = state_ref[...]
        cross = jnp.dot(q, state.astype(jnp.bfloat16), preferred_element_type=jnp.float32)
        kz = (k.astype(jnp.float32) * zeta).astype(jnp.bfloat16)
        kv = lax.dot_general(kz, v, (((0,), (0,)), ((), ())), preferred_element_type=jnp.float32)
        state_ref[...] = state * g_chunk + kv
        inner = (s * decay).astype(jnp.bfloat16)
        o = jnp.dot(inner, v, preferred_element_type=jnp.float32) + cross * xi
        if prev is not None:
            finish(*prev)
        prev = (t, o)
    finish(*prev)


def _retention(h_main):
    C = RET_CHUNK
    rows = C * RET_CHUNKS_PER_STEP
    ns = SEQ // rows
    return pl.pallas_call(
        _retention_kernel,
        grid=(RET_HEADS, BATCH, ns),
        in_specs=[
            pl.BlockSpec((rows, RET_DK), lambda h, b, c: (b * ns + c, MAIN_RQ // RET_DK + h)),
            pl.BlockSpec((rows, RET_DK), lambda h, b, c: (b * ns + c, MAIN_RK // RET_DK + h)),
            pl.BlockSpec((rows, RET_DV), lambda h, b, c: (b * ns + c, MAIN_RV // RET_DV + h)),
            pl.BlockSpec((rows, RET_DV), lambda h, b, c: (b * ns + c, MAIN_RG // RET_DV + h)),
        ],
        out_specs=pl.BlockSpec((rows, RET_DV), lambda h, b, c: (b * ns + c, h)),
        out_shape=jax.ShapeDtypeStruct((TOKENS, RET_W), jnp.bfloat16),
        scratch_shapes=[
            pltpu.VMEM((RET_DK, RET_DV), jnp.float32),
            pltpu.VMEM((C, C), jnp.float32),
            pltpu.VMEM((C, LANES), jnp.float32),
            pltpu.VMEM((C, LANES), jnp.float32),
            pltpu.VMEM((8, RET_DV), jnp.float32),
        ],
        compiler_params=_cparams(("arbitrary", "arbitrary", "arbitrary")),
        name="retention",
    )(h_main, h_main, h_main, h_main)


def _rms_norm(x, g):
    return x * lax.rsqrt(jnp.mean(x * x, axis=-1, keepdims=True) + RMS_EPS) * g


def _mla_prep_kernel(x_ref, ws_ref, qg_ref, kvg_ref, wuq_ref, wukv_ref, cos_ref, sin_ref,
                     q_ref, k_ref, v_ref):
    xb = x_ref[...].astype(jnp.bfloat16)
    hs = jnp.dot(xb, ws_ref[...], preferred_element_type=jnp.float32)
    mq = hs[:, :MLA_Q_RANK]
    mkv = hs[:, MLA_Q_RANK: MLA_Q_RANK + MLA_KV_RANK]
    mkr = hs[:, MLA_Q_RANK + MLA_KV_RANK:]
    cos = cos_ref[...]
    sin = sin_ref[...]

    qn = _rms_norm(mq, qg_ref[...]).astype(jnp.bfloat16)
    kvn = _rms_norm(mkv, kvg_ref[...]).astype(jnp.bfloat16)
    qm = jnp.dot(qn, wuq_ref[...], preferred_element_type=jnp.float32)
    kv = jnp.dot(kvn, wukv_ref[...], preferred_element_type=jnp.float32)

    kpe = (mkr * cos + pltpu.roll(mkr, HALF, axis=1) * sin).astype(k_ref.dtype)
    scale = (MLA_NOPE + MLA_ROPE) ** -0.5 * LOG2E
    lane = lax.broadcasted_iota(jnp.int32, (x_ref.shape[0], LANES), 1)
    second = (lane // (HALF // 2)) % 2
    nope_w = MLA_HEADS * MLA_NOPE
    for p in range(MLA_HEADS // 2):
        pe = qm[:, nope_w + p * LANES: nope_w + (p + 1) * LANES]
        pe = (pe * cos + pltpu.roll(pe, HALF, axis=1) * sin) * scale
        for e in range(2):
            h = 2 * p + e
            base = h * 2 * LANES
            q_ref[:, base: base + LANES] = (qm[:, h * LANES: (h + 1) * LANES] * scale).astype(q_ref.dtype)
            q_ref[:, base + LANES: base + 2 * LANES] = jnp.where(second == e, pe, 0.0).astype(q_ref.dtype)
            k_ref[:, base: base + LANES] = kv[:, h * LANES: (h + 1) * LANES].astype(k_ref.dtype)
            k_ref[:, base + LANES: base + 2 * LANES] = kpe
    v_ref[...] = kv[:, nope_w:].astype(v_ref.dtype)


def _mla_prep(x, w_small, q_g, kv_g, w_uq, w_ukv, cos_p, sin_p):
    tm = PREP_TM
    full = lambda a: _resident(a.shape)
    qk_w = MLA_HEADS * 2 * LANES
    return pl.pallas_call(
        _mla_prep_kernel,
        grid=(TOKENS // tm,),
        in_specs=[
            pl.BlockSpec((tm, D_MODEL), lambda i: (i, 0)),
            full(w_small), full(q_g), full(kv_g), full(w_uq), full(w_ukv),
            pl.BlockSpec((tm, LANES), lambda i: (i, 0)),
            pl.BlockSpec((tm, LANES), lambda i: (i, 0)),
        ],
        out_specs=[
            pl.BlockSpec((tm, qk_w), lambda i: (i, 0)),
            pl.BlockSpec((tm, qk_w), lambda i: (i, 0)),
            pl.BlockSpec((tm, MLA_W), lambda i: (i, 0)),
        ],
        out_shape=[
            jax.ShapeDtypeStruct((TOKENS, qk_w), jnp.bfloat16),
            jax.ShapeDtypeStruct((TOKENS, qk_w), jnp.bfloat16),
            jax.ShapeDtypeStruct((TOKENS, MLA_W), jnp.bfloat16),
        ],
        compiler_params=_cparams(("parallel",)),
        name="mla_prep",
    )(x, w_small, q_g, kv_g, w_uq, w_ukv, cos_p, sin_p)


def _mla_attn_kernel(q_ref, k_ref, v_ref, g_ref, o_ref):
    row = lax.broadcasted_iota(jnp.int32, (ATT_TQ, ATT_TQ), 0)
    col = lax.broadcasted_iota(jnp.int32, (ATT_TQ, ATT_TQ), 1)
    v_aug = jnp.concatenate([v_ref[...], jnp.ones((SEQ, LANES), jnp.bfloat16)], axis=1)

    def scores_softmax(i):
        r0 = i * ATT_TQ
        n = r0 + ATT_TQ
        s = lax.dot_general(q_ref[r0:n, :], k_ref[0:n, :], (((1,), (1,)), ((), ())),
                            preferred_element_type=jnp.float32)
        diag = jnp.where(col <= row, s[:, r0:], NEG)
        s = diag if i == 0 else jnp.concatenate([s[:, :r0], diag], axis=1)
        m = jnp.max(s, axis=-1, keepdims=True)
        return jnp.exp2(s - m).astype(jnp.bfloat16)

    def weighted_values(i, p):
        r0 = i * ATT_TQ
        n = r0 + ATT_TQ
        acc = jnp.dot(p, v_aug[0:n, :], preferred_element_type=jnp.float32)
        gate = _silu(g_ref[r0:n, :].astype(jnp.float32))
        o_ref[r0:n, :] = (acc[:, :MLA_V] / acc[:, MLA_V:] * gate).astype(o_ref.dtype)

    pending = []
    for i in reversed(range(SEQ // ATT_TQ)):
        pending.append((i, scores_softmax(i)))
        if len(pending) > ATT_LOOKAHEAD:
            weighted_values(*pending.pop(0))
    for item in pending:
        weighted_values(*item)


def _mla_attention(q_full, k_full, v_m, h_main):
    qk_d = 2 * LANES
    return pl.pallas_call(
        _mla_attn_kernel,
        grid=(BATCH, MLA_HEADS),
        in_specs=[
            pl.BlockSpec((SEQ, qk_d), lambda b, h: (b, h)),
            pl.BlockSpec((SEQ, qk_d), lambda b, h: (b, h)),
            pl.BlockSpec((SEQ, MLA_V), lambda b, h: (b, h)),
            pl.BlockSpec((SEQ, MLA_V), lambda b, h: (b, MAIN_MG // MLA_V + h)),
        ],
        out_specs=pl.BlockSpec((SEQ, MLA_V), lambda b, h: (b, h)),
        out_shape=jax.ShapeDtypeStruct((TOKENS, MLA_W), jnp.bfloat16),
        compiler_params=_cparams(("parallel", "parallel")),
        name="mla_attention",
    )(q_full, k_full, v_m, h_main)


def _out_ln_kernel(*refs, feature_major):
    n_acts = len(feature_major)
    a_refs = refs[:n_acts]
    w_ref, x_ref, g_ref, b_ref, o_ref = refs[n_acts:]
    for r0 in range(0, o_ref.shape[0], OUT_SUB):
        y = None
        k0 = 0
        for a_ref, fm in zip(a_refs, feature_major):
            if fm:
                kw = a_ref.shape[0]
                t = lax.dot_general(a_ref[:, r0: r0 + OUT_SUB], w_ref[k0: k0 + kw, :], (((0,), (0,)), ((), ())),
                                    preferred_element_type=jnp.float32)
            else:
                kw = a_ref.shape[1]
                t = jnp.dot(a_ref[r0: r0 + OUT_SUB, :], w_ref[k0: k0 + kw, :], preferred_element_type=jnp.float32)
            y = t if y is None else y + t
            k0 += kw
        z = DN_ALPHA * x_ref[r0: r0 + OUT_SUB, :] + y
        mu = jnp.mean(z, axis=-1, keepdims=True)
        zc = z - mu
        var = jnp.mean(zc * zc, axis=-1, keepdims=True)
        o_ref[r0: r0 + OUT_SUB, :] = zc * lax.rsqrt(var + LN_EPS) * g_ref[...] + b_ref[...]


def _out_ln(acts, w, x, ln_g, ln_b, name, feature_major=None, tm=OUT_TM):
    feature_major = tuple(feature_major or (False,) * len(acts))
    in_specs = [pl.BlockSpec((a.shape[0], tm), lambda i: (0, i)) if fm else
                pl.BlockSpec((tm, a.shape[1]), lambda i: (i, 0)) for a, fm in zip(acts, feature_major)]
    in_specs += [_resident(w.shape), pl.BlockSpec((tm, D_MODEL), lambda i: (i, 0)),
                 _resident(ln_g.shape), _resident(ln_b.shape)]
    return pl.pallas_call(
        functools.partial(_out_ln_kernel, feature_major=feature_major),
        grid=(TOKENS // tm,),
        in_specs=in_specs,
        out_specs=pl.BlockSpec((tm, D_MODEL), lambda i: (i, 0)),
        out_shape=jax.ShapeDtypeStruct((TOKENS, D_MODEL), jnp.float32),
        compiler_params=_cparams(("parallel",)),
        name=name,
    )(*acts, w, x, ln_g, ln_b)


def _split3(x):
    hi = x.astype(jnp.bfloat16)
    r1 = x - hi.astype(jnp.float32)
    mid = r1.astype(jnp.bfloat16)
    lo = (r1 - mid.astype(jnp.float32)).astype(jnp.bfloat16)
    return hi, mid, lo


def _moba_kernel(q_ref, k_ref, vt_ref, gt_ref, o_ref):
    L = MOBA_BLOCK
    NB = MOBA_NB
    means = [jnp.mean(k_ref[n * L: (n + 1) * L, :].astype(jnp.float32), axis=0, keepdims=True) for n in range(NB)]
    km = jnp.concatenate(means, axis=0)
    parts = [p.astype(jnp.float32) for p in _split3(km)]
    km3 = jnp.concatenate(parts + [jnp.zeros_like(km)], axis=0)
    mlane = lax.broadcasted_iota(jnp.int32, km3.shape, 1)
    msecond = (mlane // (HALF // 2)) % 2
    km_pair = jnp.concatenate([jnp.where(msecond == e, km3, 0.0) for e in range(2)], axis=0).astype(jnp.bfloat16)

    q = q_ref[...]
    g_pair = lax.dot_general(km_pair, q, (((1,), (1,)), ((), ())), preferred_element_type=jnp.float32)
    lane = lax.broadcasted_iota(jnp.int32, (SEQ, LANES), 1)
    second = (lane // (HALF // 2)) % 2
    blk = lax.broadcasted_iota(jnp.int32, (NB, SEQ), 0)
    qblk = lax.broadcasted_iota(jnp.int32, (NB, SEQ), 1) // L
    past = blk < qblk
    krow = lax.broadcasted_iota(jnp.int32, (L, L), 0)
    qcol = lax.broadcasted_iota(jnp.int32, (L, L), 1)
    hd = MOBA_DH
    qes, sels = [], []
    for e in range(2):
        qe = jnp.where(second == e, q, jnp.zeros_like(q))
        g3 = g_pair[e * 4 * NB: (e + 1) * 4 * NB]
        gate = jnp.where(past, g3[0:NB] + g3[NB: 2 * NB] + g3[2 * NB: 3 * NB], NEG)
        rank = jnp.zeros((NB, SEQ), jnp.float32)
        for n in range(NB - 1):
            gn = gate[n: n + 1, :]
            rank = rank + jnp.where(gn > gate, 1.0, jnp.where(gn == gate, jnp.where(blk > n, 1.0, 0.0), 0.0))
        qes.append(qe)
        sels.append(jnp.where(past, jnp.where(rank < float(MOBA_TOPK), 1.0, 0.0), 0.0))

    ones = jnp.ones((BF16_SUBLANES, SEQ), jnp.bfloat16)
    vt_aug = [jnp.concatenate([vt_ref[e * hd: (e + 1) * hd, :], ones], axis=0) for e in range(2)]

    run_max, run_acc = {}, {}

    def block_scores(e, i, j):
        r0 = i * L
        st = lax.dot_general(k_ref[j * L: (j + 1) * L, :], qes[e][r0: r0 + L, :], (((1,), (1,)), ((), ())),
                             preferred_element_type=jnp.float32)
        if j == i:
            st = jnp.where(krow <= qcol, st, NEG)
            m_new = jnp.max(st, axis=0, keepdims=True)
            run_max[(e, i)] = m_new
            return jnp.exp2(st - m_new).astype(jnp.bfloat16), None
        keep = sels[e][j: j + 1, r0: r0 + L] > 0.0
        bm = jnp.max(st, axis=0, keepdims=True)
        m_new = jnp.maximum(run_max[(e, i)], jnp.where(keep, bm, NEG))
        alpha = jnp.exp2(run_max[(e, i)] - m_new)
        run_max[(e, i)] = m_new
        return jnp.exp2(st - jnp.where(keep, m_new, -NEG)).astype(jnp.bfloat16), alpha

    def block_values(e, i, j, last, p, alpha):
        r0 = i * L
        pv = jnp.dot(vt_aug[e][:, j * L: (j + 1) * L], p, preferred_element_type=jnp.float32)
        acc = pv if alpha is None else alpha * run_acc[(e, i)] + pv
        run_acc[(e, i)] = acc
        if last:
            gate_t = _silu(gt_ref[e * hd: (e + 1) * hd, r0: r0 + L].astype(jnp.float32))
            o_ref[e * hd: (e + 1) * hd, r0: r0 + L] = (acc[:hd] / acc[hd: hd + 1] * gate_t).astype(o_ref.dtype)

    chains = [(e, i) for i in reversed(range(NB)) for e in range(2)]
    steps = []
    for s in range(NB):
        for e, i in chains:
            if s <= i:
                steps.append((e, i, i if s == 0 else s - 1, s == i))
    pending = []
    for step in steps:
        pending.append(step + block_scores(*step[:3]))
        if len(pending) > MOBA_LOOKAHEAD:
            block_values(*pending.pop(0))
    for item in pending:
        block_values(*item)


def _moba(h_qk, h_vg_t):
    npair = MOBA_HEADS // 2
    return pl.pallas_call(
        _moba_kernel,
        grid=(BATCH, npair),
        in_specs=[
            pl.BlockSpec((SEQ, LANES), lambda b, p: (b, p)),
            pl.BlockSpec((SEQ, LANES), lambda b, p: (b, npair + p)),
            pl.BlockSpec((LANES, SEQ), lambda b, p: (p, b)),
            pl.BlockSpec((LANES, SEQ), lambda b, p: (npair + p, b)),
        ],
        out_specs=pl.BlockSpec((LANES, SEQ), lambda b, p: (p, b)),
        out_shape=jax.ShapeDtypeStruct((MOBA_W, TOKENS), jnp.bfloat16),
        compiler_params=_cparams(("parallel", "parallel")),
        name="moba_attention",
    )(h_qk, h_qk, h_vg_t, h_vg_t)


def kernel(x, positions, w_in_even, q_norm_even, w_uq_even, kv_norm_even, w_ukv_even, w_out_even, w_in_odd,
           w_out_odd, ln_g, ln_b):
    bf = jnp.bfloat16
    x0 = x.reshape(TOKENS, D_MODEL)
    pos = positions.reshape(TOKENS, 1)

    inv_r = (1.0 / (ROPE_THETA ** jnp.linspace(0.0, 1.0, RET_DK // 2, dtype=jnp.float32))).reshape(1, LANES)
    inv_rope = 1.0 / (ROPE_THETA ** (jnp.arange(0, MOBA_DH, 2, dtype=jnp.float32) / MOBA_DH))
    inv_p = jnp.tile(inv_rope, LANES // (MOBA_DH // 2)).reshape(1, LANES)
    pos4 = jnp.repeat(positions.reshape(TOKENS // ROT_TOKENS_PER_ROW, ROT_TOKENS_PER_ROW),
                      LANES // ROT_TOKENS_PER_ROW, axis=1)
    cos_r, sin_r, cos_p, sin_p = _tables(pos, pos4, inv_r, inv_p)

    def permute_groups(w, src):
        k, n = w.shape
        width = len(src)
        sel = np.zeros((width, width), np.float32)
        sel[src, np.arange(width)] = 1.0
        return jnp.einsum("kgc,cd->kgd", w.reshape(k, n // width, width), sel).reshape(k, n)

    quarter = HALF // 2
    pair_src = np.concatenate([np.arange(0, quarter), np.arange(HALF, HALF + quarter),
                               np.arange(quarter, HALF), np.arange(HALF + quarter, LANES)])
    pair_layout = lambda w: permute_groups(w, pair_src)

    w_in = w_in_even[0]
    o = np.cumsum((0, 1024, 1024, RET_W, RET_W, MLA_Q_RANK, MLA_KV_RANK, MLA_ROPE, MLA_W))
    w_rqk = permute_groups(w_in[:, :o[2]],
                           np.concatenate([np.arange(0, RET_DK, 2), np.arange(1, RET_DK, 2)])).astype(bf)
    w_in_bf = w_in.astype(bf)
    w_mg = w_in_bf[:, o[7]: o[8]]
    mkr = w_in_bf[:, o[6]: o[7]].reshape(D_MODEL, 2, 1, MLA_ROPE // 2)
    kr_dup = jnp.broadcast_to(mkr, (D_MODEL, 2, 2, MLA_ROPE // 2)).reshape(D_MODEL, LANES)
    w_small = jnp.concatenate([w_in_bf[:, o[4]: o[6]], kr_dup], axis=1)
    w_uq = w_uq_even[0].reshape(MLA_Q_RANK, MLA_HEADS, MLA_NOPE + MLA_ROPE)
    w_uq_p = jnp.concatenate([w_uq[:, :, :MLA_NOPE].reshape(MLA_Q_RANK, -1),
                              pair_layout(w_uq[:, :, MLA_NOPE:].reshape(MLA_Q_RANK, -1))], axis=1).astype(bf)
    w_ukv = w_ukv_even[0].reshape(MLA_KV_RANK, MLA_HEADS, MLA_NOPE + MLA_V)
    w_ukv_p = jnp.concatenate([w_ukv[:, :, :MLA_NOPE].reshape(MLA_KV_RANK, -1),
                               w_ukv[:, :, MLA_NOPE:].reshape(MLA_KV_RANK, -1)], axis=1).astype(bf)
    w_io = w_in_odd[0]
    w_qk = jnp.concatenate([pair_layout(w_io[:, :MOBA_W]), pair_layout(w_io[:, MOBA_W: 2 * MOBA_W])],
                           axis=1).astype(bf)
    w_vg_t = w_io[:, 2 * MOBA_W:].T.astype(bf)

    h_main = _project_even(x0, w_rqk, w_in_bf, w_mg, cos_r, sin_r)
    ret_out = _retention(h_main)
    q_full, k_full, v_m = _mla_prep(x0, w_small, q_norm_even[0].reshape(1, -1), kv_norm_even[0].reshape(1, -1),
                                    w_uq_p, w_ukv_p, cos_p, sin_p)
    mla_out = _mla_attention(q_full, k_full, v_m, h_main)
    x1 = _out_ln([ret_out, mla_out], w_out_even[0].astype(bf), x0,
                 ln_g[0].reshape(1, -1), ln_b[0].reshape(1, -1), name="out_ln_even", tm=2 * OUT_TM)

    h_qk, h_vg_t = _project_odd(x1, w_qk, w_vg_t, cos_p, sin_p)
    moba_out_t = _moba(h_qk, h_vg_t)
    x2 = _out_ln([moba_out_t], w_out_odd[0].astype(bf), x1, ln_g[1].reshape(1, -1), ln_b[1].reshape(1, -1),
                 name="out_ln_odd", feature_major=(True,), tm=2 * OUT_TM)
    return x2.reshape(BATCH, SEQ, D_MODEL)
```
